```python
import math
import jax
import jax.numpy as jnp
from jax import lax
import numpy as np

D_MODEL = 1024
BATCH = 4
SEQ = 4096
DEPTH = 4

CHUNK = 64
N_MIXERS = 2
CONV_WIDTH = 4
EPS = 1e-6

D_RNN = D_MODEL
LRU_BLOCKS = 8
LRU_BLOCK_W = D_RNN // LRU_BLOCKS
RG_C = 8.0

GDN_HEAD_DIM = 128
GDN_HEADS = max(4, D_MODEL // GDN_HEAD_DIM)
GDN_DK = GDN_HEAD_DIM
GDN_DV = GDN_HEAD_DIM
GDN_HK = GDN_HEADS * GDN_DK
GDN_HV = GDN_HEADS * GDN_DV
GDN_PROJ = 2 * GDN_HK + 2 * GDN_HV + 2 * GDN_HEADS

kernel_name = "hybrid_rglru_gdn_adaln_trunk"


def rmsnorm(x, g):
    xf = x.astype(jnp.float32)
    y = xf * lax.rsqrt(jnp.mean(xf * xf, axis=-1, keepdims=True) + EPS)
    return (y * g.astype(jnp.float32)).astype(x.dtype)


def l2norm(x):
    return x * lax.rsqrt(jnp.sum(x * x, axis=-1, keepdims=True) + EPS)


def causal_dwconv(x, w):
    k = w.shape[0]
    s = x.shape[1]
    xp = jnp.pad(x, ((0, 0), (k - 1, 0), (0, 0)))
    out = xp[:, 0:s] * w[0]
    for j in range(1, k):
        out = out + xp[:, j:j + s] * w[j]
    return out


def _lin_combine(e1, e2):
    a1, b1 = e1
    a2, b2 = e2
    return a1 * a2, a2 * b1 + b2


def rglru_block(h, in_w, conv_w, conv_b, gate_w, gate_b, lam, out_w):
    bsz, s, _ = h.shape
    proj = h @ in_w
    xb, zg = proj[..., :D_RNN], proj[..., D_RNN:]
    xb = causal_dwconv(xb, conv_w) + conv_b
    xf = xb.astype(jnp.float32)
    xg = xf.reshape(bsz, s, LRU_BLOCKS, LRU_BLOCK_W)
    pre = jnp.einsum('bsni,knij->kbsnj', xg, gate_w.astype(jnp.float32))
    pre = pre.reshape(2, bsz, s, D_RNN) + gate_b.astype(jnp.float32)[:, None, None, :]
    gates = jax.nn.sigmoid(pre)
    r_t, i_t = gates[0], gates[1]
    log_a = -RG_C * r_t * jax.nn.softplus(-lam.astype(jnp.float32))
    a_t = jnp.exp(log_a)
    mult = jnp.sqrt(-jnp.expm1(2.0 * log_a))
    b_t = mult * (i_t * xf)
    _, hseq = lax.associative_scan(_lin_combine, (a_t, b_t), axis=1)
    y = hseq.astype(h.dtype) * jax.nn.silu(zg)
    return y @ out_w


def chunk_gated_delta_rule(q, k, v, g, beta):
    bsz, s, nh, dk = q.shape
    dv = v.shape[-1]
    n = s // CHUNK

    def to_chunks(t):
        return t.reshape(bsz, n, CHUNK, nh, -1).transpose(0, 3, 1, 2, 4)

    q, k, v = to_chunks(q), to_chunks(k), to_chunks(v)
    g = g.reshape(bsz, n, CHUNK, nh).transpose(0, 3, 1, 2)
    beta = beta.reshape(bsz, n, CHUNK, nh).transpose(0, 3, 1, 2)
    g = jnp.cumsum(g, axis=-1)
    idx = jnp.arange(CHUNK)
    causal = idx[:, None] >= idx[None, :]
    strict = idx[:, None] > idx[None, :]
    diff = g[..., :, None] - g[..., None, :]
    decay_mask = jnp.exp(jnp.where(causal, diff, -jnp.inf))
    k_beta = k * beta[..., None]
    v_beta = v * beta[..., None]
    a_mat = jnp.where(strict, jnp.einsum('bhncd,bhnmd->bhncm', k_beta, k) * decay_mask, 0.0)
    eye = jnp.eye(CHUNK, dtype=jnp.float32)
    t_mat = lax.linalg.triangular_solve(a_mat + eye, jnp.broadcast_to(eye, a_mat.shape),
                                        left_side=True, lower=True)
    w = jnp.einsum('bhncm,bhnmd->bhncd', t_mat, k_beta * jnp.exp(g)[..., None])
    u = jnp.einsum('bhncm,bhnmd->bhncd', t_mat, v_beta)
    attn = jnp.where(causal, jnp.einsum('bhncd,bhnmd->bhncm', q, k) * decay_mask, 0.0)
    g_last = g[..., -1]
    q_dec = q * jnp.exp(g)[..., None]
    k_dec = k * jnp.exp(g_last[..., None] - g)[..., None]
    e_last = jnp.exp(g_last)

    def step(state, inp):
        w_c, u_c, attn_c, q_c, k_c, e_c = inp
        v_new = u_c - jnp.einsum('bhcd,bhde->bhce', w_c, state)
        o_c = jnp.einsum('bhcd,bhde->bhce', q_c, state) + jnp.einsum('bhcm,bhme->bhce', attn_c, v_new)
        state = state * e_c[..., None, None] + jnp.einsum('bhcd,bhce->bhde', k_c, v_new)
        return state, o_c

    def lead(t):
        return jnp.moveaxis(t, 2, 0)

    s0 = jnp.zeros((bsz, nh, dk, dv), jnp.float32)
    _, o = lax.scan(step, s0, (lead(w), lead(u), lead(attn), lead(q_dec), lead(k_dec), lead(e_last)))
    return o.transpose(1, 0, 3, 2, 4).reshape(bsz, s, nh, dv)


def gated_deltanet_block(h, in_w, conv_w, a_log, dt_bias, onorm_g, out_w):
    bsz, s, _ = h.shape
    proj = h @ in_w
    o1 = 2 * GDN_HK + GDN_HV
    o2 = o1 + GDN_HV
    qkv = jax.nn.silu(causal_dwconv(proj[..., :o1], conv_w))
    z = proj[..., o1:o2]
    a_in = proj[..., o2:o2 + GDN_HEADS]
    b_in = proj[..., o2 + GDN_HEADS:]
    qkv = qkv.astype(jnp.float32)
    q = qkv[..., :GDN_HK].reshape(bsz, s, GDN_HEADS, GDN_DK)
    k = qkv[..., GDN_HK:2 * GDN_HK].reshape(bsz, s, GDN_HEADS, GDN_DK)
    v = qkv[..., 2 * GDN_HK:].reshape(bsz, s, GDN_HEADS, GDN_DV)
    q = l2norm(q) * (GDN_DK ** -0.5)
    k = l2norm(k)
    beta = jax.nn.sigmoid(b_in.astype(jnp.float32))
    g = -jnp.exp(a_log.astype(jnp.float32)) * jax.nn.softplus(
        a_in.astype(jnp.float32) + dt_bias.astype(jnp.float32))
    o = chunk_gated_delta_rule(q, k, v, g, beta)
    o = o * lax.rsqrt(jnp.mean(o * o, axis=-1, keepdims=True) + EPS) * onorm_g.astype(jnp.float32)
    o = o.astype(h.dtype) * jax.nn.silu(z.reshape(bsz, s, GDN_HEADS, GDN_DV))
    return o.reshape(bsz, s, GDN_HV) @ out_w


def setup_inputs(seed: int = 0) -> dict:
    key = jax.random.key(seed)
    ks = jax.random.split(key, 24)
    n_a = (DEPTH + 1) // 2
    n_b = DEPTH // 2
    f32 = jnp.float32
    nrm = lambda k, shp, sc: jax.random.normal(k, shp, f32) * sc
    x = jax.random.normal(ks[0], (BATCH, SEQ, D_MODEL), f32)
    c = jax.random.normal(ks[1], (BATCH, D_MODEL), f32)
    ada_w = nrm(ks[2], (DEPTH, D_MODEL, 3 * D_MODEL), 0.5 * D_MODEL ** -0.5)
    ada_b = nrm(ks[3], (DEPTH, 3 * D_MODEL), 0.02)
    norm_g = 1.0 + nrm(ks[4], (DEPTH, D_MODEL), 0.02)
    final_g = 1.0 + nrm(ks[5], (D_MODEL,), 0.02)
    lru_in_w = nrm(ks[6], (n_a, D_MODEL, 2 * D_RNN), D_MODEL ** -0.5)
    lru_conv_w = nrm(ks[7], (n_a, CONV_WIDTH, D_RNN), CONV_WIDTH ** -0.5)
    lru_conv_b = nrm(ks[8], (n_a, D_RNN), 0.01)
    lru_gate_w = nrm(ks[9], (n_a, 2, LRU_BLOCKS, LRU_BLOCK_W, LRU_BLOCK_W), LRU_BLOCK_W ** -0.5)
    lru_gate_b = nrm(ks[10], (n_a, 2, D_RNN), 0.01)
    a_target = jax.random.uniform(ks[11], (n_a, D_RNN), f32, 0.9, 0.999)
    s_lam = a_target ** (1.0 / RG_C)
    lru_lambda = jnp.log(s_lam) - jnp.log1p(-s_lam)
    lru_out_w = nrm(ks[12], (n_a, D_RNN, D_MODEL), D_RNN ** -0.5)
    gdn_in_w = nrm(ks[13], (n_b, D_MODEL, GDN_PROJ), D_MODEL ** -0.5)
    gdn_conv_w = nrm(ks[14], (n_b, CONV_WIDTH, 2 * GDN_HK + GDN_HV), CONV_WIDTH ** -0.5)
    gdn_a_log = jnp.log(jax.random.uniform(ks[15], (n_b, GDN_HEADS), f32, 1.0, 16.0))
    dt = jnp.exp(jax.random.uniform(ks[16], (n_b, GDN_HEADS), f32, math.log(1e-3), math.log(1e-1)))
    gdn_dt_bias = dt + jnp.log(-jnp.expm1(-dt))
    gdn_onorm_g = 1.0 + nrm(ks[17], (n_b, GDN_DV), 0.02)
    gdn_out_w = nrm(ks[18], (n_b, GDN_HV, D_MODEL), GDN_HV ** -0.5)
    return {"x": x, "c": c, "ada_w": ada_w, "ada_b": ada_b, "norm_g": norm_g, "final_g": final_g,
            "lru_in_w": lru_in_w, "lru_conv_w": lru_conv_w, "lru_conv_b": lru_conv_b,
            "lru_gate_w": lru_gate_w, "lru_gate_b": lru_gate_b, "lru_lambda": lru_lambda,
            "lru_out_w": lru_out_w,
            "gdn_in_w": gdn_in_w, "gdn_conv_w": gdn_conv_w, "gdn_a_log": gdn_a_log,
            "gdn_dt_bias": gdn_dt_bias, "gdn_onorm_g": gdn_onorm_g, "gdn_out_w": gdn_out_w}


def reference(x, c, ada_w, ada_b, norm_g, final_g,
              lru_in_w, lru_conv_w, lru_conv_b, lru_gate_w, lru_gate_b, lru_lambda, lru_out_w,
              gdn_in_w, gdn_conv_w, gdn_a_log, gdn_dt_bias, gdn_onorm_g, gdn_out_w):
    c_act = jax.nn.silu(c)
    for i in range(DEPTH):
        cond = c_act @ ada_w[i] + ada_b[i]
        shift = cond[:, None, :D_MODEL]
        scale = cond[:, None, D_MODEL:2 * D_MODEL]
        gate = cond[:, None, 2 * D_MODEL:]
        h = rmsnorm(x, norm_g[i]) * (1.0 + scale) + shift
        j = i // N_MIXERS
        if i % N_MIXERS == 0:
            out = rglru_block(h, lru_in_w[j], lru_conv_w[j], lru_conv_b[j], lru_gate_w[j],
                              lru_gate_b[j], lru_lambda[j], lru_out_w[j])
        else:
            out = gated_deltanet_block(h, gdn_in_w[j], gdn_conv_w[j], gdn_a_log[j],
                                       gdn_dt_bias[j], gdn_onorm_g[j], gdn_out_w[j])
        x = x + gate * out
    return rmsnorm(x, final_g)
```

```python
import functools

import numpy as np
import jax
import jax.numpy as jnp
from jax import lax
from jax.experimental import pallas as pl
from jax.experimental.pallas import tpu as pltpu

F32 = jnp.float32
BF16 = jnp.bfloat16

EPS = 1e-6
RG_C = 8.0
CONV_WIDTH = 4
SUBLANES = 8
LANES = 128
GDN_CHUNK = 64
HEAD_DIM = 128
LRU_BLOCK_W = 128
VMEM_LIMIT_BYTES = 56 * 1024 * 1024


def _sigmoid(v):
    return jax.nn.sigmoid(v)


def _softplus(v):
    return jnp.maximum(v, 0.0) + jnp.log1p(jnp.exp(-jnp.abs(v)))


def _bdot(a, b):
    return jnp.dot(a.astype(BF16), b.astype(BF16), preferred_element_type=F32)


def _bdot_nt(a, b):
    return lax.dot_general(a.astype(BF16), b.astype(BF16), (((1,), (1,)), ((), ())),
                           preferred_element_type=F32)


def _bdot_tn(a, b):
    return lax.dot_general(a.astype(BF16), b.astype(BF16), (((0,), (0,)), ((), ())),
                           preferred_element_type=F32)


def _split3(v):
    v1 = v.astype(BF16)
    r1 = v - v1.astype(F32)
    v2 = r1.astype(BF16)
    v3 = (r1 - v2.astype(F32)).astype(BF16)
    return v1, v2, v3


def _modulated_rmsnorm(x, g, shift, scale):
    ms = jnp.mean(x * x, axis=-1, keepdims=True)
    return (x * lax.rsqrt(ms + EPS)) * g * (1.0 + scale) + shift


def _cond_kernel(c_ref, w_ref, b_ref, o_ref):
    c = c_ref[...]
    ca = c * _sigmoid(c)
    o_ref[0] = _bdot(ca, w_ref[0]) + b_ref[0]


def _cond(c, ada_w, ada_b):
    depth, d, d3 = ada_w.shape
    bsz = c.shape[0]
    bp = -(-bsz // SUBLANES) * SUBLANES
    c8 = jnp.pad(c, ((0, bp - bsz), (0, 0)))
    tn = d
    out = pl.pallas_call(
        _cond_kernel,
        grid=(depth, d3 // tn),
        in_specs=[pl.BlockSpec((bp, d), lambda i, j: (0, 0)),
                  pl.BlockSpec((1, d, tn), lambda i, j: (i, 0, j)),
                  pl.BlockSpec((1, 1, tn), lambda i, j: (i, 0, j))],
        out_specs=pl.BlockSpec((1, bp, tn), lambda i, j: (i, 0, j)),
        out_shape=jax.ShapeDtypeStruct((depth, bp, d3), F32),
        name="adaln_cond",
    )(c8, ada_w, ada_b.reshape(depth, 1, d3))
    return out[:, :bsz].reshape(depth, bsz, 3, d)


def _lru_kernel(x_ref, cond_ref, ng_ref, perm_ref, permt_ref, inw_ref, cw_ref, cb_ref, gw_ref,
                gb_ref, lam_ref, outw_ref, o_ref,
                tail_ref, carry_ref, a_ref, b_ref, hl_ref, p_ref, *, tm, d):
    g_rows = tm // SUBLANES
    t = pl.program_id(1)

    @pl.when(t == 0)
    def _():
        tail_ref[...] = jnp.zeros_like(tail_ref)
        carry_ref[...] = jnp.zeros_like(carry_ref)

    x = x_ref[0]
    shift = cond_ref[0, 0:1, :]
    scale = cond_ref[0, 1:2, :]
    gate = cond_ref[0, 2:3, :]
    h = _modulated_rmsnorm(x, ng_ref[...], shift, scale)
    hp = jnp.dot(perm_ref[...], h.astype(BF16), preferred_element_type=F32).astype(BF16)
    proj = jnp.dot(hp, inw_ref[...], preferred_element_type=F32)
    xb = proj[:, :d]
    zg = proj[:, d:]

    n_tail = (CONV_WIDTH - 1) * SUBLANES
    cur_tail = xb[tm - n_tail:, :]
    prev_tail = tail_ref[...]
    sub = lax.broadcasted_iota(jnp.int32, (SUBLANES, d), 0)
    heads = []
    for m in range(CONV_WIDTH - 1):
        cur = pltpu.roll(cur_tail[m * SUBLANES:(m + 1) * SUBLANES, :], 1, 0)
        prv = pltpu.roll(prev_tail[m * SUBLANES:(m + 1) * SUBLANES, :], 1, 0)
        heads.append(jnp.where(sub == 0, prv, cur))
    tail_ref[...] = cur_tail
    ext = jnp.concatenate(heads + [xb], axis=0)
    xf = cb_ref[...]
    for k in range(CONV_WIDTH):
        off = n_tail - k * SUBLANES
        xf = xf + cw_ref[CONV_WIDTH - 1 - k:CONV_WIDTH - k, :] * ext[off:off + tm, :]

    xfb = xf.astype(BF16)
    nblk = d // LRU_BLOCK_W
    pre_r, pre_i = [], []
    for n in range(nblk):
        pre = jnp.dot(xfb[:, n * LRU_BLOCK_W:(n + 1) * LRU_BLOCK_W], gw_ref[n],
                      preferred_element_type=F32)
        pre_r.append(pre[:, :LRU_BLOCK_W])
        pre_i.append(pre[:, LRU_BLOCK_W:])
    r_t = _sigmoid(jnp.concatenate(pre_r, axis=1) + gb_ref[0:1, :])
    i_t = _sigmoid(jnp.concatenate(pre_i, axis=1) + gb_ref[1:2, :])
    log_a = (-RG_C * _softplus(-lam_ref[...])) * r_t
    a_t = jnp.exp(log_a)
    th = jnp.tanh(log_a)
    mult = jnp.sqrt(-2.0 * th / (1.0 - th))
    a_ref[...] = a_t
    b_ref[...] = mult * (i_t * xf)

    def scan_body(i, carry):
        hl, p = carry
        rows = pl.ds(pl.multiple_of(i * SUBLANES, SUBLANES), SUBLANES)
        a_i = a_ref[rows, :]
        hl = a_i * hl + b_ref[rows, :]
        p = a_i * p
        hl_ref[rows, :] = hl
        p_ref[rows, :] = p
        return hl, p

    h_end, p_end = lax.fori_loop(0, g_rows, scan_body,
                                 (jnp.zeros((SUBLANES, d), F32), jnp.ones((SUBLANES, d), F32)),
                                 unroll=4)
    c_in = jnp.broadcast_to(carry_ref[SUBLANES - 1:SUBLANES, :], (SUBLANES, d))
    c_sub = c_in
    for _ in range(SUBLANES - 1):
        c_sub = jnp.where(sub == 0, c_in, pltpu.roll(h_end + p_end * c_sub, 1, 0))
    carry_ref[...] = h_end + p_end * c_sub
    hseq = (hl_ref[...].reshape(g_rows, SUBLANES, d)
            + p_ref[...].reshape(g_rows, SUBLANES, d) * c_sub[None]).reshape(tm, d)

    y = hseq * (zg * _sigmoid(zg))
    yn = jnp.dot(permt_ref[...], y.astype(BF16), preferred_element_type=F32).astype(BF16)
    out = jnp.dot(yn, outw_ref[...], preferred_element_type=F32)
    o_ref[0] = x + gate * out


def _time_permutation(tm):
    g_rows = tm // SUBLANES
    r = np.arange(tm)
    src = (r % SUBLANES) * g_rows + r // SUBLANES
    p = np.zeros((tm, tm), np.float32)
    p[r, src] = 1.0
    return p


def _lru_layer(x, cond_i, norm_g, in_w, conv_w, conv_b, gate_w, gate_b, lam, out_w, *, tm):
    bsz, s, d = x.shape
    nblk = d // LRU_BLOCK_W
    perm = _time_permutation(tm)
    gw = jnp.concatenate([gate_w[0], gate_w[1]], axis=-1).astype(BF16)
    const = lambda *shape: pl.BlockSpec(shape, lambda b, t: (0,) * len(shape))
    kern = functools.partial(_lru_kernel, tm=tm, d=d)
    return pl.pallas_call(
        kern,
        grid=(bsz, s // tm),
        in_specs=[pl.BlockSpec((1, tm, d), lambda b, t: (b, t, 0)),
                  pl.BlockSpec((1, 3, d), lambda b, t: (b, 0, 0)),
                  const(1, d), const(tm, tm), const(tm, tm), const(d, 2 * d), const(CONV_WIDTH, d),
                  const(1, d), const(nblk, LRU_BLOCK_W, 2 * LRU_BLOCK_W), const(2, d), const(1, d),
                  const(d, d)],
        out_specs=pl.BlockSpec((1, tm, d), lambda b, t: (b, t, 0)),
        out_shape=jax.ShapeDtypeStruct((bsz, s, d), F32),
        scratch_shapes=[pltpu.VMEM(((CONV_WIDTH - 1) * SUBLANES, d), F32),
                        pltpu.VMEM((SUBLANES, d), F32),
                        pltpu.VMEM((tm, d), F32), pltpu.VMEM((tm, d), F32),
                        pltpu.VMEM((tm, d), F32), pltpu.VMEM((tm, d), F32)],
        compiler_params=pltpu.CompilerParams(dimension_semantics=("arbitrary", "arbitrary"),
                                             vmem_limit_bytes=VMEM_LIMIT_BYTES),
        name="rglru_layer",
    )(x, cond_i, norm_g.reshape(1, d), jnp.asarray(perm, BF16), jnp.asarray(perm.T, BF16),
      in_w.astype(BF16), conv_w, conv_b.reshape(1, d), gw, gate_b, lam.reshape(1, d),
      out_w.astype(BF16))


def _gdn_kernel(x_ref, cond_ref, ng_ref, wqkv_ref, wz_ref, wab_ref, wabt_ref, cw_ref, alog_ref,
                dtb_ref, alogr_ref, dtbr_ref, tri_ref, trit_ref, og_ref, outw_ref, fg_ref, o_ref,
                hist_ref, state_ref, q_ref, k_ref, v_ref, gc_ref, gcr_ref, beta_ref, oacc_ref,
                *, tm, d, nheads, final_norm):
    c_len = GDN_CHUNK
    nchunks = tm // c_len
    hk = nheads * HEAD_DIM
    t = pl.program_id(1)

    @pl.when(t == 0)
    def _():
        hist_ref[...] = jnp.zeros_like(hist_ref)
        state_ref[...] = jnp.zeros_like(state_ref)

    x = x_ref[0]
    shift = cond_ref[0, 0:1, :]
    scale = cond_ref[0, 1:2, :]
    gate = cond_ref[0, 2:3, :]
    hb = _modulated_rmsnorm(x, ng_ref[...], shift, scale).astype(BF16)
    qkv_pre = jnp.dot(hb, wqkv_ref[...], preferred_element_type=F32)
    z = jnp.dot(hb, wz_ref[...], preferred_element_type=F32)
    ab = jnp.dot(hb, wab_ref[...], preferred_element_type=F32)
    abt = lax.dot_general(wabt_ref[...], hb, (((1,), (1,)), ((), ())),
                          preferred_element_type=F32)

    ext = jnp.concatenate([hist_ref[...], qkv_pre], axis=0)
    hist_ref[...] = qkv_pre[tm - SUBLANES:, :]
    conv = cw_ref[CONV_WIDTH - 1:CONV_WIDTH, :] * qkv_pre
    for k in range(1, CONV_WIDTH):
        off = SUBLANES - k
        conv = conv + cw_ref[CONV_WIDTH - 1 - k:CONV_WIDTH - k, :] * ext[off:off + tm, :]
    qkv = conv * _sigmoid(conv)

    for h in range(nheads):
        lo = h * HEAD_DIM
        qh = qkv[:, lo:lo + HEAD_DIM]
        kh = qkv[:, hk + lo:hk + lo + HEAD_DIM]
        q_ref[h] = qh * (lax.rsqrt(jnp.sum(qh * qh, axis=-1, keepdims=True) + EPS)
                         * (HEAD_DIM ** -0.5))
        k_ref[h] = kh * lax.rsqrt(jnp.sum(kh * kh, axis=-1, keepdims=True) + EPS)
        v_ref[h] = qkv[:, 2 * hk + lo:2 * hk + lo + HEAD_DIM]

    g_col = -jnp.exp(alog_ref[...]) * _softplus(ab[:, :LANES] + dtb_ref[...])
    g_row = -jnp.exp(alogr_ref[...]) * _softplus(abt + dtbr_ref[...])
    beta_ref[...] = _sigmoid(ab[:, LANES:])
    gcol_parts = _split3(g_col)
    grow_parts = _split3(g_row)
    gc = sum(jnp.dot(tri_ref[...], p, preferred_element_type=F32) for p in gcol_parts)
    gcr = sum(jnp.dot(p, trit_ref[...], preferred_element_type=F32) for p in grow_parts)
    gc_ref[...] = gc
    for c in range(nchunks):
        gcr_ref[c] = gcr[:, c * c_len:(c + 1) * c_len]

    rr = lax.broadcasted_iota(jnp.int32, (2 * c_len, 2 * c_len), 0)
    cc = lax.broadcasted_iota(jnp.int32, (2 * c_len, 2 * c_len), 1)
    same = (rr >= c_len) == (cc >= c_len)
    causal = same & (rr >= cc)
    strict = same & (rr > cc)
    eye = (rr == cc).astype(F32)
    merge_masks = []
    blk = 1
    while blk < c_len:
        merge_masks.append(((rr ^ cc) < 2 * blk) & ((rr & blk) != 0) & ((cc & blk) == 0))
        blk *= 2

    def chunk_body(c, carry):
        rows = pl.ds(pl.multiple_of(c * c_len, c_len), c_len)
        gcc = gc_ref[rows, :]
        eg_c = jnp.exp(gcc)
        g_last = gcc[c_len - 1:c_len, :]
        kd_c = jnp.exp(g_last - gcc)
        e_last = jnp.exp(g_last)
        beta_c = beta_ref[rows, :]
        be_c = beta_c * eg_c
        grow_c = gcr_ref[c]

        def col(arr, h):
            return jnp.broadcast_to(arr[:, h:h + 1], (c_len, HEAD_DIM))

        for hp in range(nheads // 2):
            h0, h1 = 2 * hp, 2 * hp + 1
            pair = lambda f: jnp.concatenate([f(h0), f(h1)], axis=0)
            q2 = pair(lambda h: q_ref[h, rows, :])
            k2 = pair(lambda h: k_ref[h, rows, :])
            v2 = pair(lambda h: v_ref[h, rows, :])
            beta2 = pair(lambda h: col(beta_c, h))
            be2 = pair(lambda h: col(be_c, h))
            eg2 = pair(lambda h: col(eg_c, h))
            kd2 = pair(lambda h: col(kd_c, h))
            gcol2 = pair(lambda h: col(gcc, h))
            grow2 = jnp.concatenate([grow_c[h0:h0 + 1, :], grow_c[h1:h1 + 1, :]], axis=1)
            decay = jnp.exp(jnp.where(causal, gcol2 - grow2, -1e30))
            kb2 = k2 * beta2
            gram = _bdot_nt(jnp.concatenate([kb2, q2], axis=0), k2)
            a_mat = jnp.where(strict, gram[:2 * c_len] * decay, 0.0)
            attn = gram[2 * c_len:] * decay
            t_mat = eye - jnp.where(merge_masks[0], a_mat, 0.0)
            for m in merge_masks[1:]:
                t_mat = t_mat - _bdot(t_mat, _bdot(jnp.where(m, a_mat, 0.0), t_mat))
            wu = _bdot(t_mat, jnp.concatenate([k2 * be2, v2 * beta2], axis=1))
            w2 = wu[:, :HEAD_DIM]
            u2 = wu[:, HEAD_DIM:]
            qd2 = q2 * eg2
            ws_parts, qs_parts = [], []
            for idx, h in enumerate((h0, h1)):
                sl = slice(idx * c_len, (idx + 1) * c_len)
                wq = _bdot(jnp.concatenate([w2[sl], qd2[sl]], axis=0), state_ref[h])
                ws_parts.append(wq[:c_len])
                qs_parts.append(wq[c_len:])
            v_new = u2 - jnp.concatenate(ws_parts, axis=0)
            o2 = jnp.concatenate(qs_parts, axis=0) + _bdot(attn, v_new)
            kdec = k2 * kd2
            for idx, h in enumerate((h0, h1)):
                sl = slice(idx * c_len, (idx + 1) * c_len)
                el = jnp.broadcast_to(e_last[:, h:h + 1], (HEAD_DIM, HEAD_DIM))
                state_ref[h] = state_ref[h] * el + _bdot_tn(kdec[sl], v_new[sl])
                oacc_ref[rows, h * HEAD_DIM:(h + 1) * HEAD_DIM] = o2[sl]
        return carry

    lax.fori_loop(0, nchunks, chunk_body, 0)

    o_all = oacc_ref[...]
    parts = []
    for h in range(nheads):
        oh = o_all[:, h * HEAD_DIM:(h + 1) * HEAD_DIM]
        parts.append(oh * lax.rsqrt(jnp.mean(oh * oh, axis=-1, keepdims=True) + EPS))
    on = jnp.concatenate(parts, axis=1) * og_ref[...]
    og = on * (z * _sigmoid(z))
    out = jnp.dot(og.astype(BF16), outw_ref[...], preferred_element_type=F32)
    xn = x + gate * out
    if final_norm:
        ms = jnp.mean(xn * xn, axis=-1, keepdims=True)
        xn = xn * lax.rsqrt(ms + EPS) * fg_ref[...]
    o_ref[0] = xn


def _block_tri(tm):
    r = np.arange(tm)
    m = (r[:, None] >= r[None, :]) & (r[:, None] // GDN_CHUNK == r[None, :] // GDN_CHUNK)
    return m.astype(np.float32)


def _gdn_layer(x, cond_i, norm_g, in_w, conv_w, a_log, dt_bias, onorm_g, out_w, final_g, *, tm,
               final_norm):
    bsz, s, d = x.shape
    nheads = a_log.shape[0]
    hk = nheads * HEAD_DIM
    nchunks = tm // GDN_CHUNK
    o1 = 3 * hk
    o2 = o1 + hk
    wqkv = in_w[:, :o1].astype(BF16)
    wz = in_w[:, o1:o2].astype(BF16)
    wa = in_w[:, o2:o2 + nheads]
    wb = in_w[:, o2 + nheads:]
    pad_l = lambda w: jnp.pad(w, ((0, 0), (0, LANES - nheads)))
    wab = jnp.concatenate([pad_l(wa), pad_l(wb)], axis=1).astype(BF16)
    wabt = jnp.concatenate([wa, wb], axis=1).T.astype(BF16)
    lane_vec = lambda v: jnp.pad(v, (0, LANES - nheads)).reshape(1, LANES)
    row_vec = lambda v: jnp.broadcast_to(jnp.pad(v, (0, nheads))[:, None], (2 * nheads, tm))
    tri = _block_tri(tm)
    const = lambda *shape: pl.BlockSpec(shape, lambda b, t: (0,) * len(shape))
    kern = functools.partial(_gdn_kernel, tm=tm, d=d, nheads=nheads, final_norm=final_norm)
    return pl.pallas_call(
        kern,
        grid=(bsz, s // tm),
        in_specs=[pl.BlockSpec((1, tm, d), lambda b, t: (b, t, 0)),
                  pl.BlockSpec((1, 3, d), lambda b, t: (b, 0, 0)),
                  const(1, d), const(d, o1), const(d, hk), const(d, 2 * LANES),
                  const(2 * nheads, d), const(CONV_WIDTH, o1), const(1, LANES), const(1, LANES),
                  const(2 * nheads, tm), const(2 * nheads, tm), const(tm, tm), const(tm, tm),
                  const(1, hk), const(hk, d), const(1, d)],
        out_specs=pl.BlockSpec((1, tm, d), lambda b, t: (b, t, 0)),
        out_shape=jax.ShapeDtypeStruct((bsz, s, d), F32),
        scratch_shapes=[pltpu.VMEM((SUBLANES, o1), F32),
                        pltpu.VMEM((nheads, HEAD_DIM, HEAD_DIM), F32),
                        pltpu.VMEM((nheads, tm, HEAD_DIM), F32),
                        pltpu.VMEM((nheads, tm, HEAD_DIM), F32),
                        pltpu.VMEM((nheads, tm, HEAD_DIM), F32),
                        pltpu.VMEM((tm, LANES), F32),
                        pltpu.VMEM((nchunks, 2 * nheads, GDN_CHUNK), F32),
                        pltpu.VMEM((tm, LANES), F32),
                        pltpu.VMEM((tm, hk), F32)],
        compiler_params=pltpu.CompilerParams(dimension_semantics=("arbitrary", "arbitrary"),
                                             vmem_limit_bytes=VMEM_LIMIT_BYTES),
        name="gdn_layer",
    )(x, cond_i, norm_g.reshape(1, d), wqkv, wz, wab, wabt, conv_w, lane_vec(a_log),
      lane_vec(dt_bias), row_vec(a_log), row_vec(dt_bias), jnp.asarray(tri, BF16),
      jnp.asarray(tri.T, BF16), jnp.tile(onorm_g, nheads).reshape(1, hk), out_w.astype(BF16),
      final_g.reshape(1, d))


def kernel(x, c, ada_w, ada_b, norm_g, final_g, lru_in_w, lru_conv_w, lru_conv_b, lru_gate_w,
           lru_gate_b, lru_lambda, lru_out_w, gdn_in_w, gdn_conv_w, gdn_a_log, gdn_dt_bias,
           gdn_onorm_g, gdn_out_w):
    depth = ada_w.shape[0]
    assert depth % 2 == 0, "layers alternate RG-LRU / DeltaNet and the last one applies the final norm"
    cond = _cond(c, ada_w, ada_b)
    for i in range(depth):
        j = i // 2
        if i % 2 == 0:
            x = _lru_layer(x, cond[i], norm_g[i], lru_in_w[j], lru_conv_w[j], lru_conv_b[j],
                           lru_gate_w[j], lru_gate_b[j], lru_lambda[j], lru_out_w[j], tm=256)
        else:
            x = _gdn_layer(x, cond[i], norm_g[i], gdn_in_w[j], gdn_conv_w[j], gdn_a_log[j],
                           gdn_dt_bias[j], gdn_onorm_g[j], gdn_out_w[j], final_g, tm=256,
                           final_norm=(i == depth - 1))
    return x
```

```python
import functools

import numpy as np
import jax
import jax.numpy as jnp
from jax import lax
from jax.experimental import pallas as pl
from jax.experimental.pallas import tpu as pltpu

F32 = jnp.float32
BF16 = jnp.bfloat16

EPS = 1e-6
RG_C = 8.0
CONV_WIDTH = 4
SUBLANES = 8
LANES = 128
GDN_CHUNK = 64
HEAD_DIM = 128
LRU_BLOCK_W = 128
VMEM_LIMIT_BYTES = 56 * 1024 * 1024


def _sigmoid(v):
    return jax.nn.sigmoid(v)


def _softplus(v):
    return jnp.maximum(v, 0.0) + jnp.log1p(jnp.exp(-jnp.abs(v)))


def _bdot(a, b):
    return jnp.dot(a.astype(BF16), b.astype(BF16), preferred_element_type=F32)


def _bmm(a, b):
    return lax.dot_general(a.astype(BF16), b.astype(BF16), (((2,), (1,)), ((0,), (0,))),
                           preferred_element_type=F32)


def _bmm_nt(a, b):
    return lax.dot_general(a.astype(BF16), b.astype(BF16), (((2,), (2,)), ((0,), (0,))),
                           preferred_element_type=F32)


def _bmm_tn(a, b):
    return lax.dot_general(a.astype(BF16), b.astype(BF16), (((1,), (1,)), ((0,), (0,))),
                           preferred_element_type=F32)


def _split3(v):
    v1 = v.astype(BF16)
    r1 = v - v1.astype(F32)
    v2 = r1.astype(BF16)
    v3 = (r1 - v2.astype(F32)).astype(BF16)
    return v1, v2, v3


def _modulated_rmsnorm(x, g, shift, scale):
    ms = jnp.mean(x * x, axis=-1, keepdims=True)
    return (x * lax.rsqrt(ms + EPS)) * g * (1.0 + scale) + shift


def _cond_kernel(c_ref, w_ref, b_ref, o_ref):
    c = c_ref[...]
    ca = c * _sigmoid(c)
    o_ref[0] = _bdot(ca, w_ref[0]) + b_ref[0]


def _cond(c, ada_w, ada_b):
    depth, d, d3 = ada_w.shape
    bsz = c.shape[0]
    bp = -(-bsz // SUBLANES) * SUBLANES
    c8 = jnp.pad(c, ((0, bp - bsz), (0, 0)))
    tn = d
    out = pl.pallas_call(
        _cond_kernel,
        grid=(depth, d3 // tn),
        in_specs=[pl.BlockSpec((bp, d), lambda i, j: (0, 0)),
                  pl.BlockSpec((1, d, tn), lambda i, j: (i, 0, j)),
                  pl.BlockSpec((1, 1, tn), lambda i, j: (i, 0, j))],
        out_specs=pl.BlockSpec((1, bp, tn), lambda i, j: (i, 0, j)),
        out_shape=jax.ShapeDtypeStruct((depth, bp, d3), F32),
        name="adaln_cond",
    )(c8, ada_w, ada_b.reshape(depth, 1, d3))
    return out[:, :bsz].reshape(depth, bsz, 3, d)


def _lru_kernel(x_ref, cond_ref, ng_ref, perm_ref, permt_ref, inw_ref, cw_ref, cb_ref, gw_ref,
                gb_ref, lam_ref, outw_ref, o_ref,
                tail_ref, carry_ref, a_ref, b_ref, hl_ref, p_ref, *, tm, d):
    g_rows = tm // SUBLANES
    t = pl.program_id(1)

    @pl.when(t == 0)
    def _():
        tail_ref[...] = jnp.zeros_like(tail_ref)
        carry_ref[...] = jnp.zeros_like(carry_ref)

    x = x_ref[0]
    shift = cond_ref[0, 0:1, :]
    scale = cond_ref[0, 1:2, :]
    gate = cond_ref[0, 2:3, :]
    h = _modulated_rmsnorm(x, ng_ref[...], shift, scale)
    hp = jnp.dot(perm_ref[...], h.astype(BF16), preferred_element_type=F32).astype(BF16)
    proj = jnp.dot(hp, inw_ref[...], preferred_element_type=F32)
    xb = proj[:, :d]
    zg = proj[:, d:]

    n_tail = (CONV_WIDTH - 1) * SUBLANES
    cur_tail = xb[tm - n_tail:, :]
    prev_tail = tail_ref[...]
    sub = lax.broadcasted_iota(jnp.int32, (SUBLANES, d), 0)
    heads = []
    for m in range(CONV_WIDTH - 1):
        cur = pltpu.roll(cur_tail[m * SUBLANES:(m + 1) * SUBLANES, :], 1, 0)
        prv = pltpu.roll(prev_tail[m * SUBLANES:(m + 1) * SUBLANES, :], 1, 0)
        heads.append(jnp.where(sub == 0, prv, cur))
    tail_ref[...] = cur_tail
    ext = jnp.concatenate(heads + [xb], axis=0)
    xf = cb_ref[...]
    for k in range(CONV_WIDTH):
        off = n_tail - k * SUBLANES
        xf = xf + cw_ref[CONV_WIDTH - 1 - k:CONV_WIDTH - k, :] * ext[off:off + tm, :]

    xfb = xf.astype(BF16)
    nblk = d // LRU_BLOCK_W
    pre_r, pre_i = [], []
    for n in range(nblk):
        pre = jnp.dot(xfb[:, n * LRU_BLOCK_W:(n + 1) * LRU_BLOCK_W], gw_ref[n],
                      preferred_element_type=F32)
        pre_r.append(pre[:, :LRU_BLOCK_W])
        pre_i.append(pre[:, LRU_BLOCK_W:])
    r_t = _sigmoid(jnp.concatenate(pre_r, axis=1) + gb_ref[0:1, :])
    i_t = _sigmoid(jnp.concatenate(pre_i, axis=1) + gb_ref[1:2, :])
    log_a = (-RG_C * _softplus(-lam_ref[...])) * r_t
    a_t = jnp.exp(log_a)
    th = jnp.tanh(log_a)
    mult = jnp.sqrt(-2.0 * th / (1.0 - th))
    a_ref[...] = a_t
    b_ref[...] = mult * (i_t * xf)

    def scan_body(i, carry):
        hl, p = carry
        rows = pl.ds(pl.multiple_of(i * SUBLANES, SUBLANES), SUBLANES)
        a_i = a_ref[rows, :]
        hl = a_i * hl + b_ref[rows, :]
        p = a_i * p
        hl_ref[rows, :] = hl
        p_ref[rows, :] = p
        return hl, p

    h_end, p_end = lax.fori_loop(0, g_rows, scan_body,
                                 (jnp.zeros((SUBLANES, d), F32), jnp.ones((SUBLANES, d), F32)),
                                 unroll=4)
    c_in = jnp.broadcast_to(carry_ref[SUBLANES - 1:SUBLANES, :], (SUBLANES, d))
    c_sub = c_in
    for _ in range(SUBLANES - 1):
        c_sub = jnp.where(sub == 0, c_in, pltpu.roll(h_end + p_end * c_sub, 1, 0))
    carry_ref[...] = h_end + p_end * c_sub
    hseq = (hl_ref[...].reshape(g_rows, SUBLANES, d)
            + p_ref[...].reshape(g_rows, SUBLANES, d) * c_sub[None]).reshape(tm, d)

    y = hseq * (zg * _sigmoid(zg))
    yn = jnp.dot(permt_ref[...], y.astype(BF16), preferred_element_type=F32).astype(BF16)
    out = jnp.dot(yn, outw_ref[...], preferred_element_type=F32)
    o_ref[0] = x + gate * out


def _time_permutation(tm):
    g_rows = tm // SUBLANES
    r = np.arange(tm)
    src = (r % SUBLANES) * g_rows + r // SUBLANES
    p = np.zeros((tm, tm), np.float32)
    p[r, src] = 1.0
    return p


def _lru_layer(x, cond_i, norm_g, in_w, conv_w, conv_b, gate_w, gate_b, lam, out_w, *, tm):
    bsz, s, d = x.shape
    nblk = d // LRU_BLOCK_W
    perm = _time_permutation(tm)
    gw = jnp.concatenate([gate_w[0], gate_w[1]], axis=-1).astype(BF16)
    const = lambda *shape: pl.BlockSpec(shape, lambda b, t: (0,) * len(shape))
    kern = functools.partial(_lru_kernel, tm=tm, d=d)
    return pl.pallas_call(
        kern,
        grid=(bsz, s // tm),
        in_specs=[pl.BlockSpec((1, tm, d), lambda b, t: (b, t, 0)),
                  pl.BlockSpec((1, 3, d), lambda b, t: (b, 0, 0)),
                  const(1, d), const(tm, tm), const(tm, tm), const(d, 2 * d), const(CONV_WIDTH, d),
                  const(1, d), const(nblk, LRU_BLOCK_W, 2 * LRU_BLOCK_W), const(2, d), const(1, d),
                  const(d, d)],
        out_specs=pl.BlockSpec((1, tm, d), lambda b, t: (b, t, 0)),
        out_shape=jax.ShapeDtypeStruct((bsz, s, d), F32),
        scratch_shapes=[pltpu.VMEM(((CONV_WIDTH - 1) * SUBLANES, d), F32),
                        pltpu.VMEM((SUBLANES, d), F32),
                        pltpu.VMEM((tm, d), F32), pltpu.VMEM((tm, d), F32),
                        pltpu.VMEM((tm, d), F32), pltpu.VMEM((tm, d), F32)],
        compiler_params=pltpu.CompilerParams(dimension_semantics=("arbitrary", "arbitrary"),
                                             vmem_limit_bytes=VMEM_LIMIT_BYTES),
        name="rglru_layer",
    )(x, cond_i, norm_g.reshape(1, d), jnp.asarray(perm, BF16), jnp.asarray(perm.T, BF16),
      in_w.astype(BF16), conv_w, conv_b.reshape(1, d), gw, gate_b, lam.reshape(1, d),
      out_w.astype(BF16))


def _gdn_kernel(x_ref, cond_ref, ng_ref, wqkv_ref, wz_ref, wab_ref, wabt_ref, cw_ref, alog_ref,
                dtb_ref, alogr_ref, dtbr_ref, tri_ref, trit_ref, og_ref, outw_ref, fg_ref, o_ref,
                hist_ref, state_ref, qp_ref, kp_ref, vp_ref, oacc_ref,
                *, nb, tm, d, nheads, final_norm):
    c_len = GDN_CHUNK
    nchunks = tm // c_len
    npairs = nheads // 2
    grp = nb * npairs
    hk = nheads * HEAD_DIM
    rows = nb * tm
    t = pl.program_id(1)

    @pl.when(t == 0)
    def _():
        hist_ref[...] = jnp.zeros_like(hist_ref)
        state_ref[...] = jnp.zeros_like(state_ref)

    hb = jnp.concatenate(
        [_modulated_rmsnorm(x_ref[b], ng_ref[...], cond_ref[b, 0:1, :], cond_ref[b, 1:2, :])
         .astype(BF16) for b in range(nb)], axis=0)
    qkv_pre = jnp.dot(hb, wqkv_ref[...], preferred_element_type=F32)
    z = jnp.dot(hb, wz_ref[...], preferred_element_type=F32)
    ab = jnp.dot(hb, wab_ref[...], preferred_element_type=F32)
    abt = lax.dot_general(wabt_ref[...], hb, (((1,), (1,)), ((), ())),
                          preferred_element_type=F32)

    convs = []
    for b in range(nb):
        cur = qkv_pre[b * tm:(b + 1) * tm, :]
        ext = jnp.concatenate([hist_ref[b], cur], axis=0)
        hist_ref[b] = cur[tm - SUBLANES:, :]
        conv = cw_ref[CONV_WIDTH - 1:CONV_WIDTH, :] * cur
        for k in range(1, CONV_WIDTH):
            off = SUBLANES - k
            conv = conv + cw_ref[CONV_WIDTH - 1 - k:CONV_WIDTH - k, :] * ext[off:off + tm, :]
        convs.append(conv)
    conv = jnp.concatenate(convs, axis=0)
    qkv = conv * _sigmoid(conv)

    for h in range(nheads):
        lo = h * HEAD_DIM
        hp, par = divmod(h, 2)
        qh = qkv[:, lo:lo + HEAD_DIM]
        kh = qkv[:, hk + lo:hk + lo + HEAD_DIM]
        qn = qh * (lax.rsqrt(jnp.sum(qh * qh, axis=-1, keepdims=True) + EPS) * (HEAD_DIM ** -0.5))
        kn = kh * lax.rsqrt(jnp.sum(kh * kh, axis=-1, keepdims=True) + EPS)
        vh = qkv[:, 2 * hk + lo:2 * hk + lo + HEAD_DIM]
        for b in range(nb):
            for c in range(nchunks):
                r0 = b * tm + c * c_len
                dst = (c, b * npairs + hp, slice(par * c_len, (par + 1) * c_len), slice(None))
                qp_ref[dst] = qn[r0:r0 + c_len, :]
                kp_ref[dst] = kn[r0:r0 + c_len, :]
                vp_ref[dst] = vh[r0:r0 + c_len, :]

    g_col = -jnp.exp(alog_ref[...]) * _softplus(ab[:, :LANES] + dtb_ref[...])
    g_row = -jnp.exp(alogr_ref[...]) * _softplus(abt + dtbr_ref[...])
    beta = _sigmoid(ab[:, LANES:])
    gcol_parts = _split3(g_col)
    grow_parts = _split3(g_row)
    gc_blocks, gcr_blocks = [], []
    for i in range(rows // LANES):
        sl = slice(i * LANES, (i + 1) * LANES)
        gc_blocks.append(sum(jnp.dot(tri_ref[...], p[sl, :], preferred_element_type=F32)
                             for p in gcol_parts))
        gcr_blocks.append(sum(jnp.dot(p[:, sl], trit_ref[...], preferred_element_type=F32)
                              for p in grow_parts))
    gc = jnp.concatenate(gc_blocks, axis=0)
    gcr = jnp.concatenate(gcr_blocks, axis=1)

    def pair_cols(arr):
        mats = []
        for c in range(nchunks):
            for b in range(nb):
                r0 = b * tm + c * c_len
                blk = arr[r0:r0 + c_len, :]
                for hp in range(npairs):
                    mats.append(jnp.concatenate(
                        [jnp.broadcast_to(blk[:, 2 * hp + par:2 * hp + par + 1], (c_len, HEAD_DIM))
                         for par in range(2)], axis=0))
        return jnp.stack(mats)

    def pair_rows(arr):
        mats = []
        for c in range(nchunks):
            for b in range(nb):
                r0 = b * tm + c * c_len
                for hp in range(npairs):
                    mats.append(jnp.concatenate(
                        [arr[2 * hp + par:2 * hp + par + 1, r0:r0 + c_len] for par in range(2)],
                        axis=1))
        return jnp.stack(mats)

    n_mat = nchunks * grp
    rr = lax.broadcasted_iota(jnp.int32, (2 * c_len, 2 * c_len), 0)
    cc = lax.broadcasted_iota(jnp.int32, (2 * c_len, 2 * c_len), 1)
    same = (rr >= c_len) == (cc >= c_len)
    causal = same & (rr >= cc)
    strict = same & (rr > cc)
    eye = (rr == cc).astype(F32)
    merge_masks = []
    blk = 1
    while blk < c_len:
        merge_masks.append(((rr ^ cc) < 2 * blk) & ((rr & blk) != 0) & ((cc & blk) == 0))
        blk *= 2

    q2 = qp_ref[...].reshape(n_mat, 2 * c_len, HEAD_DIM)
    k2 = kp_ref[...].reshape(n_mat, 2 * c_len, HEAD_DIM)
    v2 = vp_ref[...].reshape(n_mat, 2 * c_len, HEAD_DIM)
    beta2 = pair_cols(beta)
    gcol2 = pair_cols(gc)
    grow2 = pair_rows(gcr)
    eg2 = jnp.exp(gcol2)
    glast2 = jnp.concatenate(
        [jnp.broadcast_to(gcol2[:, (par + 1) * c_len - 1:(par + 1) * c_len, :],
                          (n_mat, c_len, HEAD_DIM)) for par in range(2)], axis=1)
    kd2 = jnp.exp(glast2 - gcol2)
    decay = jnp.exp(jnp.where(causal[None], gcol2 - grow2, -1e30))
    kb2 = k2 * beta2
    gram = _bmm_nt(jnp.concatenate([kb2, q2], axis=1), k2)
    a_mat = jnp.where(strict[None], gram[:, :2 * c_len] * decay, 0.0)
    attn = gram[:, 2 * c_len:] * decay
    t_mat = eye[None] - jnp.where(merge_masks[0][None], a_mat, 0.0)
    for m in merge_masks[1:]:
        t_mat = t_mat - _bmm(t_mat, _bmm(jnp.where(m[None], a_mat, 0.0), t_mat))
    wu = _bmm(t_mat, jnp.concatenate([kb2 * eg2, v2 * beta2], axis=2))
    w2 = wu[:, :, :HEAD_DIM]
    u2 = wu[:, :, HEAD_DIM:]
    qd2 = q2 * eg2
    kdec2 = k2 * kd2

    for c in range(nchunks):
        sl = slice(c * grp, (c + 1) * grp)
        vn_parts, qs_parts = [], []
        for par in range(2):
            rs = slice(par * c_len, (par + 1) * c_len)
            wq = _bmm(jnp.concatenate([w2[sl, rs], qd2[sl, rs]], axis=1), state_ref[:, par])
            vn_parts.append(u2[sl, rs] - wq[:, :c_len])
            qs_parts.append(wq[:, c_len:])
        v_new = jnp.concatenate(vn_parts, axis=1)
        o2 = jnp.concatenate(qs_parts, axis=1) + _bmm(attn[sl], v_new)
        for par in range(2):
            rs = slice(par * c_len, (par + 1) * c_len)
            e_last = jnp.broadcast_to(eg2[sl, (par + 1) * c_len - 1:(par + 1) * c_len, :],
                                      (grp, HEAD_DIM, HEAD_DIM))
            state_ref[:, par] = state_ref[:, par] * e_last + _bmm_tn(kdec2[sl, rs], v_new[:, rs])
        for b in range(nb):
            for hp in range(npairs):
                for par in range(2):
                    h = 2 * hp + par
                    oacc_ref[b * tm + c * c_len:b * tm + (c + 1) * c_len,
                             h * HEAD_DIM:(h + 1) * HEAD_DIM] = (
                                 o2[b * npairs + hp, par * c_len:(par + 1) * c_len, :])

    o_all = oacc_ref[...]
    parts = []
    for h in range(nheads):
        oh = o_all[:, h * HEAD_DIM:(h + 1) * HEAD_DIM]
        parts.append(oh * lax.rsqrt(jnp.mean(oh * oh, axis=-1, keepdims=True) + EPS))
    on = jnp.concatenate(parts, axis=1) * og_ref[...]
    og = on * (z * _sigmoid(z))
    out = jnp.dot(og.astype(BF16), outw_ref[...], preferred_element_type=F32)
    for b in range(nb):
        xn = x_ref[b] + cond_ref[b, 2:3, :] * out[b * tm:(b + 1) * tm, :]
        if final_norm:
            ms = jnp.mean(xn * xn, axis=-1, keepdims=True)
            xn = xn * lax.rsqrt(ms + EPS) * fg_ref[...]
        o_ref[b] = xn


def _pair_tri():
    r = np.arange(LANES)
    m = (r[:, None] >= r[None, :]) & (r[:, None] // GDN_CHUNK == r[None, :] // GDN_CHUNK)
    return m.astype(np.float32)


def _gdn_layer(x, cond_i, norm_g, in_w, conv_w, a_log, dt_bias, onorm_g, out_w, final_g, *, nb, tm,
               final_norm):
    bsz, s, d = x.shape
    nheads = a_log.shape[0]
    hk = nheads * HEAD_DIM
    nchunks = tm // GDN_CHUNK
    npairs = nheads // 2
    rows = nb * tm
    o1 = 3 * hk
    o2 = o1 + hk
    wqkv = in_w[:, :o1].astype(BF16)
    wz = in_w[:, o1:o2].astype(BF16)
    wa = in_w[:, o2:o2 + nheads]
    wb = in_w[:, o2 + nheads:]
    pad_l = lambda w: jnp.pad(w, ((0, 0), (0, LANES - nheads)))
    wab = jnp.concatenate([pad_l(wa), pad_l(wb)], axis=1).astype(BF16)
    wabt = jnp.concatenate([wa, wb], axis=1).T.astype(BF16)
    lane_vec = lambda v: jnp.pad(v, (0, LANES - nheads)).reshape(1, LANES)
    row_vec = lambda v: jnp.broadcast_to(jnp.pad(v, (0, nheads))[:, None], (2 * nheads, rows))
    tri = _pair_tri()
    const = lambda *shape: pl.BlockSpec(shape, lambda b, t: (0,) * len(shape))
    kern = functools.partial(_gdn_kernel, nb=nb, tm=tm, d=d, nheads=nheads, final_norm=final_norm)
    pair_stack = pltpu.VMEM((nchunks, nb * npairs, 2 * GDN_CHUNK, HEAD_DIM), F32)
    return pl.pallas_call(
        kern,
        grid=(bsz // nb, s // tm),
        in_specs=[pl.BlockSpec((nb, tm, d), lambda b, t: (b, t, 0)),
                  pl.BlockSpec((nb, 3, d), lambda b, t: (b, 0, 0)),
                  const(1, d), const(d, o1), const(d, hk), const(d, 2 * LANES),
                  const(2 * nheads, d), const(CONV_WIDTH, o1), const(1, LANES), const(1, LANES),
                  const(2 * nheads, rows), const(2 * nheads, rows), const(LANES, LANES),
                  const(LANES, LANES), const(1, hk), const(hk, d), const(1, d)],
        out_specs=pl.BlockSpec((nb, tm, d), lambda b, t: (b, t, 0)),
        out_shape=jax.ShapeDtypeStruct((bsz, s, d), F32),
        scratch_shapes=[pltpu.VMEM((nb, SUBLANES, o1), F32),
                        pltpu.VMEM((nb * npairs, 2, HEAD_DIM, HEAD_DIM), F32),
                        pair_stack, pair_stack, pair_stack,
                        pltpu.VMEM((rows, hk), F32)],
        compiler_params=pltpu.CompilerParams(dimension_semantics=("arbitrary", "arbitrary"),
                                             vmem_limit_bytes=VMEM_LIMIT_BYTES),
        name="gdn_layer",
    )(x, cond_i, norm_g.reshape(1, d), wqkv, wz, wab, wabt, conv_w, lane_vec(a_log),
      lane_vec(dt_bias), row_vec(a_log), row_vec(dt_bias), jnp.asarray(tri, BF16),
      jnp.asarray(tri.T, BF16), jnp.tile(onorm_g, nheads).reshape(1, hk), out_w.astype(BF16),
      final_g.reshape(1, d))


def kernel(x, c, ada_w, ada_b, norm_g, final_g, lru_in_w, lru_conv_w, lru_conv_b, lru_gate_w,
           lru_gate_b, lru_lambda, lru_out_w, gdn_in_w, gdn_conv_w, gdn_a_log, gdn_dt_bias,
           gdn_onorm_g, gdn_out_w):
    depth = ada_w.shape[0]
    assert depth % 2 == 0, "layers alternate RG-LRU / DeltaNet and the last one applies the final norm"
    cond = _cond(c, ada_w, ada_b)
    for i in range(depth):
        j = i // 2
        if i % 2 == 0:
            x = _lru_layer(x, cond[i], norm_g[i], lru_in_w[j], lru_conv_w[j], lru_conv_b[j],
                           lru_gate_w[j], lru_gate_b[j], lru_lambda[j], lru_out_w[j], tm=256)
        else:
            x = _gdn_layer(x, cond[i], norm_g[i], gdn_in_w[j], gdn_conv_w[j], gdn_a_log[j],
                           gdn_dt_bias[j], gdn_onorm_g[j], gdn_out_w[j], final_g, nb=1, tm=256,
                           final_norm=(i == depth - 1))
    return x
```

```python
import functools

import numpy as np
import jax
import jax.numpy as jnp
from jax import lax
from jax.experimental import pallas as pl
from jax.experimental.pallas import tpu as pltpu

F32 = jnp.float32
BF16 = jnp.bfloat16

EPS = 1e-6
RG_C = 8.0
CONV_WIDTH = 4
SUBLANES = 8
LANES = 128
GDN_CHUNK = 64
HEAD_DIM = 128
LRU_BLOCK_W = 128
LRU_COL_BLOCK = 256
VMEM_LIMIT_BYTES = 56 * 1024 * 1024


def _sigmoid(v):
    return jax.nn.sigmoid(v)


def _softplus(v):
    return jnp.maximum(v, 0.0) + jnp.log1p(jnp.exp(-jnp.abs(v)))


def _bdot(a, b):
    return jnp.dot(a.astype(BF16), b.astype(BF16), preferred_element_type=F32)


def _bmm(a, b):
    return lax.dot_general(a.astype(BF16), b.astype(BF16), (((2,), (1,)), ((0,), (0,))),
                           preferred_element_type=F32)


def _bmm_nt(a, b):
    return lax.dot_general(a.astype(BF16), b.astype(BF16), (((2,), (2,)), ((0,), (0,))),
                           preferred_element_type=F32)


def _bmm_tn(a, b):
    return lax.dot_general(a.astype(BF16), b.astype(BF16), (((1,), (1,)), ((0,), (0,))),
                           preferred_element_type=F32)


def _split3(v):
    v1 = v.astype(BF16)
    r1 = v - v1.astype(F32)
    v2 = r1.astype(BF16)
    v3 = (r1 - v2.astype(F32)).astype(BF16)
    return v1, v2, v3


def _modulated_rmsnorm(x, g, shift, scale):
    ms = jnp.mean(x * x, axis=-1, keepdims=True)
    return (x * lax.rsqrt(ms + EPS)) * g * (1.0 + scale) + shift


def _cond_kernel(c_ref, w_ref, b_ref, o_ref):
    c = c_ref[...]
    ca = c * _sigmoid(c)
    o_ref[0] = _bdot(ca, w_ref[0]) + b_ref[0]


def _cond(c, ada_w, ada_b):
    depth, d, d3 = ada_w.shape
    bsz = c.shape[0]
    bp = -(-bsz // SUBLANES) * SUBLANES
    c8 = jnp.pad(c, ((0, bp - bsz), (0, 0)))
    tn = d
    out = pl.pallas_call(
        _cond_kernel,
        grid=(depth, d3 // tn),
        in_specs=[pl.BlockSpec((bp, d), lambda i, j: (0, 0)),
                  pl.BlockSpec((1, d, tn), lambda i, j: (i, 0, j)),
                  pl.BlockSpec((1, 1, tn), lambda i, j: (i, 0, j))],
        out_specs=pl.BlockSpec((1, bp, tn), lambda i, j: (i, 0, j)),
        out_shape=jax.ShapeDtypeStruct((depth, bp, d3), F32),
        name="adaln_cond",
    )(c8, ada_w, ada_b.reshape(depth, 1, d3))
    return out[:, :bsz].reshape(depth, bsz, 3, d)


def _lru_kernel(x_ref, cond_ref, ng_ref, perm_ref, permt_ref, inw_ref, cw_ref, cb_ref, gw_ref,
                gb_ref, lam_ref, outw_ref, o_ref, tail_ref, carry_ref, *, nb, ns, tm, d):
    t = pl.program_id(1)

    @pl.when(t == 0)
    def _():
        tail_ref[...] = jnp.zeros_like(tail_ref)
        carry_ref[...] = jnp.zeros_like(carry_ref)

    ncol = d // LRU_COL_BLOCK
    tiles = [(b, s) for s in range(ns) for b in range(nb)]

    def permuted_input(b, s):
        x = x_ref[b, pl.ds(s * tm, tm), :]
        h = _modulated_rmsnorm(x, ng_ref[...], cond_ref[b, 0:1, :], cond_ref[b, 1:2, :])
        return jnp.dot(perm_ref[...], h.astype(BF16), preferred_element_type=F32).astype(BF16)

    def in_proj(hp, n):
        cols = slice(n * LRU_COL_BLOCK, (n + 1) * LRU_COL_BLOCK)
        zcols = slice(d + n * LRU_COL_BLOCK, d + (n + 1) * LRU_COL_BLOCK)
        return (jnp.dot(hp, inw_ref[:, cols], preferred_element_type=F32),
                jnp.dot(hp, inw_ref[:, zcols], preferred_element_type=F32))

    def out_proj(tile, yn, n):
        b, s = tile
        rows = pl.ds(s * tm, tm)
        cols = slice(n * LRU_COL_BLOCK, (n + 1) * LRU_COL_BLOCK)
        out = jnp.dot(yn, outw_ref[:, cols], preferred_element_type=F32)
        o_ref[b, rows, cols] = x_ref[b, rows, cols] + cond_ref[b, 2:3, cols] * out

    def unpermute(y_blocks):
        y = jnp.concatenate(y_blocks, axis=1)
        return jnp.dot(permt_ref[...], y, preferred_element_type=F32).astype(BF16)

    hp = permuted_input(*tiles[0])
    proj = [in_proj(hp, n) for n in range(ncol)]
    prev, yn_prev = None, None
    for idx, (b, s) in enumerate(tiles):
        nxt = tiles[idx + 1] if idx + 1 < len(tiles) else None
        if nxt is not None:
            hp = permuted_input(*nxt)
        proj_next, y_blocks = [], []
        for n in range(ncol):
            if nxt is not None:
                proj_next.append(in_proj(hp, n))
            if prev is not None:
                out_proj(prev, yn_prev, n)
            cols = slice(n * LRU_COL_BLOCK, (n + 1) * LRU_COL_BLOCK)
            y_blocks.append(_lru_recurrence_block(
                proj[n][0], proj[n][1], cw_ref.at[:, cols], cb_ref.at[:, cols],
                gw_ref.at[pl.ds(n * (LRU_COL_BLOCK // LRU_BLOCK_W), LRU_COL_BLOCK // LRU_BLOCK_W)],
                gb_ref.at[:, cols], lam_ref.at[:, cols], tail_ref.at[b, :, cols],
                carry_ref.at[b, :, cols], tm=tm))
        proj = proj_next
        prev, yn_prev = (b, s), unpermute(y_blocks)
    for n in range(ncol):
        out_proj(prev, yn_prev, n)


def _lru_recurrence_block(xb, zg, cw_ref, cb_ref, gw_ref, gb_ref, lam_ref, tail_ref, carry_ref, *, tm):
    g_rows = tm // SUBLANES
    w = xb.shape[1]
    n_tail = (CONV_WIDTH - 1) * SUBLANES
    cur_tail = xb[tm - n_tail:, :]
    prev_tail = tail_ref[...]
    sub = lax.broadcasted_iota(jnp.int32, (SUBLANES, w), 0)
    heads = []
    for m in range(CONV_WIDTH - 1):
        cur = pltpu.roll(cur_tail[m * SUBLANES:(m + 1) * SUBLANES, :], 1, 0)
        prv = pltpu.roll(prev_tail[m * SUBLANES:(m + 1) * SUBLANES, :], 1, 0)
        heads.append(jnp.where(sub == 0, prv, cur))
    tail_ref[...] = cur_tail
    ext = jnp.concatenate(heads + [xb], axis=0)
    xf = cb_ref[...]
    for k in range(CONV_WIDTH):
        off = n_tail - k * SUBLANES
        xf = xf + cw_ref[CONV_WIDTH - 1 - k:CONV_WIDTH - k, :] * ext[off:off + tm, :]

    xfb = xf.astype(BF16)
    pre_r, pre_i = [], []
    for n in range(w // LRU_BLOCK_W):
        pre = jnp.dot(xfb[:, n * LRU_BLOCK_W:(n + 1) * LRU_BLOCK_W], gw_ref[n],
                      preferred_element_type=F32)
        pre_r.append(pre[:, :LRU_BLOCK_W])
        pre_i.append(pre[:, LRU_BLOCK_W:])
    r_t = _sigmoid(jnp.concatenate(pre_r, axis=1) + gb_ref[0:1, :])
    i_t = _sigmoid(jnp.concatenate(pre_i, axis=1) + gb_ref[1:2, :])
    log_a = (-RG_C * _softplus(-lam_ref[...])) * r_t
    a_t = jnp.exp(log_a)
    th = jnp.tanh(log_a)
    mult = jnp.sqrt(-2.0 * th / (1.0 - th))
    b_t = mult * (i_t * xf)

    h_loc = jnp.zeros((SUBLANES, w), F32)
    p_loc = jnp.ones((SUBLANES, w), F32)
    h_rows, p_rows = [], []
    for i in range(g_rows):
        a_i = a_t[i * SUBLANES:(i + 1) * SUBLANES, :]
        h_loc = a_i * h_loc + b_t[i * SUBLANES:(i + 1) * SUBLANES, :]
        p_loc = a_i * p_loc
        h_rows.append(h_loc)
        p_rows.append(p_loc)
    c_in = jnp.broadcast_to(carry_ref[SUBLANES - 1:SUBLANES, :], (SUBLANES, w))
    c_sub = c_in
    for _ in range(SUBLANES - 1):
        c_sub = jnp.where(sub == 0, c_in, pltpu.roll(h_loc + p_loc * c_sub, 1, 0))
    carry_ref[...] = h_loc + p_loc * c_sub
    hseq = jnp.concatenate([hr + pr * c_sub for hr, pr in zip(h_rows, p_rows)], axis=0)
    return (hseq * (zg * _sigmoid(zg))).astype(BF16)


def _time_permutation(tm):
    g_rows = tm // SUBLANES
    r = np.arange(tm)
    src = (r % SUBLANES) * g_rows + r // SUBLANES
    p = np.zeros((tm, tm), np.float32)
    p[r, src] = 1.0
    return p


def _lru_layer(x, cond_i, norm_g, in_w, conv_w, conv_b, gate_w, gate_b, lam, out_w, *, nb, ns, tm):
    bsz, s, d = x.shape
    nblk = d // LRU_BLOCK_W
    perm = _time_permutation(tm)
    gw = jnp.concatenate([gate_w[0], gate_w[1]], axis=-1).astype(BF16)
    const = lambda *shape: pl.BlockSpec(shape, lambda b, t: (0,) * len(shape))
    kern = functools.partial(_lru_kernel, nb=nb, ns=ns, tm=tm, d=d)
    return pl.pallas_call(
        kern,
        grid=(bsz // nb, s // (ns * tm)),
        in_specs=[pl.BlockSpec((nb, ns * tm, d), lambda b, t: (b, t, 0)),
                  pl.BlockSpec((nb, 3, d), lambda b, t: (b, 0, 0)),
                  const(1, d), const(tm, tm), const(tm, tm), const(d, 2 * d), const(CONV_WIDTH, d),
                  const(1, d), const(nblk, LRU_BLOCK_W, 2 * LRU_BLOCK_W), const(2, d), const(1, d),
                  const(d, d)],
        out_specs=pl.BlockSpec((nb, ns * tm, d), lambda b, t: (b, t, 0)),
        out_shape=jax.ShapeDtypeStruct((bsz, s, d), F32),
        scratch_shapes=[pltpu.VMEM((nb, (CONV_WIDTH - 1) * SUBLANES, d), F32),
                        pltpu.VMEM((nb, SUBLANES, d), F32)],
        compiler_params=pltpu.CompilerParams(dimension_semantics=("arbitrary", "arbitrary"),
                                             vmem_limit_bytes=VMEM_LIMIT_BYTES),
        name="rglru_layer",
    )(x, cond_i, norm_g.reshape(1, d), jnp.asarray(perm, BF16), jnp.asarray(perm.T, BF16),
      in_w.astype(BF16), conv_w, conv_b.reshape(1, d), gw, gate_b, lam.reshape(1, d),
      out_w.astype(BF16))


def _gdn_kernel(x_ref, cond_ref, ng_ref, wqkv_ref, wz_ref, wab_ref, wabt_ref, cw_ref, alog_ref,
                dtb_ref, alogr_ref, dtbr_ref, tri_ref, trit_ref, og_ref, outw_ref, fg_ref, o_ref,
                hist_ref, state_ref, qp_ref, kp_ref, vp_ref, oacc_ref,
                *, nb, tm, d, nheads, final_norm):
    c_len = GDN_CHUNK
    nchunks = tm // c_len
    npairs = nheads // 2
    grp = nb * npairs
    hk = nheads * HEAD_DIM
    rows = nb * tm
    t = pl.program_id(1)

    @pl.when(t == 0)
    def _():
        hist_ref[...] = jnp.zeros_like(hist_ref)
        state_ref[...] = jnp.zeros_like(state_ref)

    hb = jnp.concatenate(
        [_modulated_rmsnorm(x_ref[b], ng_ref[...], cond_ref[b, 0:1, :], cond_ref[b, 1:2, :])
         .astype(BF16) for b in range(nb)], axis=0)
    qkv_pre = jnp.dot(hb, wqkv_ref[...], preferred_element_type=F32)
    z = jnp.dot(hb, wz_ref[...], preferred_element_type=F32)
    ab = jnp.dot(hb, wab_ref[...], preferred_element_type=F32)
    abt = lax.dot_general(wabt_ref[...], hb, (((1,), (1,)), ((), ())),
                          preferred_element_type=F32)

    convs = []
    for b in range(nb):
        cur = qkv_pre[b * tm:(b + 1) * tm, :]
        ext = jnp.concatenate([hist_ref[b], cur], axis=0)
        hist_ref[b] = cur[tm - SUBLANES:, :]
        conv = cw_ref[CONV_WIDTH - 1:CONV_WIDTH, :] * cur
        for k in range(1, CONV_WIDTH):
            off = SUBLANES - k
            conv = conv + cw_ref[CONV_WIDTH - 1 - k:CONV_WIDTH - k, :] * ext[off:off + tm, :]
        convs.append(conv)
    conv = jnp.concatenate(convs, axis=0)
    qkv = conv * _sigmoid(conv)

    for h in range(nheads):
        lo = h * HEAD_DIM
        hp, par = divmod(h, 2)
        qh = qkv[:, lo:lo + HEAD_DIM]
        kh = qkv[:, hk + lo:hk + lo + HEAD_DIM]
        qn = qh * (lax.rsqrt(jnp.sum(qh * qh, axis=-1, keepdims=True) + EPS) * (HEAD_DIM ** -0.5))
        kn = kh * lax.rsqrt(jnp.sum(kh * kh, axis=-1, keepdims=True) + EPS)
        vh = qkv[:, 2 * hk + lo:2 * hk + lo + HEAD_DIM]
        for b in range(nb):
            for c in range(nchunks):
                r0 = b * tm + c * c_len
                dst = (c, b * npairs + hp, slice(par * c_len, (par + 1) * c_len), slice(None))
                qp_ref[dst] = qn[r0:r0 + c_len, :]
                kp_ref[dst] = kn[r0:r0 + c_len, :]
                vp_ref[dst] = vh[r0:r0 + c_len, :]

    g_col = -jnp.exp(alog_ref[...]) * _softplus(ab[:, :LANES] + dtb_ref[...])
    g_row = -jnp.exp(alogr_ref[...]) * _softplus(abt + dtbr_ref[...])
    beta = _sigmoid(ab[:, LANES:])
    gcol_parts = _split3(g_col)
    grow_parts = _split3(g_row)
    gc_blocks, gcr_blocks = [], []
    for i in range(rows // LANES):
        sl = slice(i * LANES, (i + 1) * LANES)
        gc_blocks.append(sum(jnp.dot(tri_ref[...], p[sl, :], preferred_element_type=F32)
                             for p in gcol_parts))
        gcr_blocks.append(sum(jnp.dot(p[:, sl], trit_ref[...], preferred_element_type=F32)
                              for p in grow_parts))
    gc = jnp.concatenate(gc_blocks, axis=0)
    gcr = jnp.concatenate(gcr_blocks, axis=1)

    def pair_cols(arr):
        mats = []
        for c in range(nchunks):
            for b in range(nb):
                r0 = b * tm + c * c_len
                blk = arr[r0:r0 + c_len, :]
                for hp in range(npairs):
                    mats.append(jnp.concatenate(
                        [jnp.broadcast_to(blk[:, 2 * hp + par:2 * hp + par + 1], (c_len, HEAD_DIM))
                         for par in range(2)], axis=0))
        return jnp.stack(mats)

    def pair_rows(arr):
        mats = []
        for c in range(nchunks):
            for b in range(nb):
                r0 = b * tm + c * c_len
                for hp in range(npairs):
                    mats.append(jnp.concatenate(
                        [arr[2 * hp + par:2 * hp + par + 1, r0:r0 + c_len] for par in range(2)],
                        axis=1))
        return jnp.stack(mats)

    n_mat = nchunks * grp
    rr = lax.broadcasted_iota(jnp.int32, (2 * c_len, 2 * c_len), 0)
    cc = lax.broadcasted_iota(jnp.int32, (2 * c_len, 2 * c_len), 1)
    same = (rr >= c_len) == (cc >= c_len)
    causal = same & (rr >= cc)
    strict = same & (rr > cc)
    eye = (rr == cc).astype(F32)
    merge_masks = []
    blk = 1
    while blk < c_len:
        merge_masks.append(((rr ^ cc) < 2 * blk) & ((rr & blk) != 0) & ((cc & blk) == 0))
        blk *= 2

    q2 = qp_ref[...].reshape(n_mat, 2 * c_len, HEAD_DIM)
    k2 = kp_ref[...].reshape(n_mat, 2 * c_len, HEAD_DIM)
    v2 = vp_ref[...].reshape(n_mat, 2 * c_len, HEAD_DIM)
    beta2 = pair_cols(beta)
    gcol2 = pair_cols(gc)
    grow2 = pair_rows(gcr)
    eg2 = jnp.exp(gcol2)
    glast2 = jnp.concatenate(
        [jnp.broadcast_to(gcol2[:, (par + 1) * c_len - 1:(par + 1) * c_len, :],
                          (n_mat, c_len, HEAD_DIM)) for par in range(2)], axis=1)
    kd2 = jnp.exp(glast2 - gcol2)
    decay = jnp.exp(jnp.where(causal[None], gcol2 - grow2, -1e30))
    kb2 = k2 * beta2
    gram = _bmm_nt(jnp.concatenate([kb2, q2], axis=1), k2)
    a_mat = jnp.where(strict[None], gram[:, :2 * c_len] * decay, 0.0)
    attn = gram[:, 2 * c_len:] * decay
    t_mat = eye[None] - jnp.where(merge_masks[0][None], a_mat, 0.0)
    for m in merge_masks[1:]:
        t_mat = t_mat - _bmm(t_mat, _bmm(jnp.where(m[None], a_mat, 0.0), t_mat))
    wu = _bmm(t_mat, jnp.concatenate([kb2 * eg2, v2 * beta2], axis=2))
    w2 = wu[:, :, :HEAD_DIM]
    u2 = wu[:, :, HEAD_DIM:]
    qd2 = q2 * eg2
    kdec2 = k2 * kd2

    for c in range(nchunks):
        sl = slice(c * grp, (c + 1) * grp)
        vn_parts, qs_parts = [], []
        for par in range(2):
            rs = slice(par * c_len, (par + 1) * c_len)
            wq = _bmm(jnp.concatenate([w2[sl, rs], qd2[sl, rs]], axis=1), state_ref[:, par])
            vn_parts.append(u2[sl, rs] - wq[:, :c_len])
            qs_parts.append(wq[:, c_len:])
        v_new = jnp.concatenate(vn_parts, axis=1)
        o2 = jnp.concatenate(qs_parts, axis=1) + _bmm(attn[sl], v_new)
        for par in range(2):
            rs = slice(par * c_len, (par + 1) * c_len)
            e_last = jnp.broadcast_to(eg2[sl, (par + 1) * c_len - 1:(par + 1) * c_len, :],
                                      (grp, HEAD_DIM, HEAD_DIM))
            state_ref[:, par] = state_ref[:, par] * e_last + _bmm_tn(kdec2[sl, rs], v_new[:, rs])
        for b in range(nb):
            for hp in range(npairs):
                for par in range(2):
                    h = 2 * hp + par
                    oacc_ref[b * tm + c * c_len:b * tm + (c + 1) * c_len,
                             h * HEAD_DIM:(h + 1) * HEAD_DIM] = (
                                 o2[b * npairs + hp, par * c_len:(par + 1) * c_len, :])

    o_all = oacc_ref[...]
    parts = []
    for h in range(nheads):
        oh = o_all[:, h * HEAD_DIM:(h + 1) * HEAD_DIM]
        parts.append(oh * lax.rsqrt(jnp.mean(oh * oh, axis=-1, keepdims=True) + EPS))
    on = jnp.concatenate(parts, axis=1) * og_ref[...]
    og = on * (z * _sigmoid(z))
    out = jnp.dot(og.astype(BF16), outw_ref[...], preferred_element_type=F32)
    for b in range(nb):
        xn = x_ref[b] + cond_ref[b, 2:3, :] * out[b * tm:(b + 1) * tm, :]
        if final_norm:
            ms = jnp.mean(xn * xn, axis=-1, keepdims=True)
            xn = xn * lax.rsqrt(ms + EPS) * fg_ref[...]
        o_ref[b] = xn


def _pair_tri():
    r = np.arange(LANES)
    m = (r[:, None] >= r[None, :]) & (r[:, None] // GDN_CHUNK == r[None, :] // GDN_CHUNK)
    return m.astype(np.float32)


def _gdn_layer(x, cond_i, norm_g, in_w, conv_w, a_log, dt_bias, onorm_g, out_w, final_g, *, nb, tm,
               final_norm):
    bsz, s, d = x.shape
    nheads = a_log.shape[0]
    hk = nheads * HEAD_DIM
    nchunks = tm // GDN_CHUNK
    npairs = nheads // 2
    rows = nb * tm
    o1 = 3 * hk
    o2 = o1 + hk
    wqkv = in_w[:, :o1].astype(BF16)
    wz = in_w[:, o1:o2].astype(BF16)
    wa = in_w[:, o2:o2 + nheads]
    wb = in_w[:, o2 + nheads:]
    pad_l = lambda w: jnp.pad(w, ((0, 0), (0, LANES - nheads)))
    wab = jnp.concatenate([pad_l(wa), pad_l(wb)], axis=1).astype(BF16)
    wabt = jnp.concatenate([wa, wb], axis=1).T.astype(BF16)
    lane_vec = lambda v: jnp.pad(v, (0, LANES - nheads)).reshape(1, LANES)
    row_vec = lambda v: jnp.broadcast_to(jnp.pad(v, (0, nheads))[:, None], (2 * nheads, rows))
    tri = _pair_tri()
    const = lambda *shape: pl.BlockSpec(shape, lambda b, t: (0,) * len(shape))
    kern = functools.partial(_gdn_kernel, nb=nb, tm=tm, d=d, nheads=nheads, final_norm=final_norm)
    pair_stack = pltpu.VMEM((nchunks, nb * npairs, 2 * GDN_CHUNK, HEAD_DIM), F32)
    return pl.pallas_call(
        kern,
        grid=(bsz // nb, s // tm),
        in_specs=[pl.BlockSpec((nb, tm, d), lambda b, t: (b, t, 0)),
                  pl.BlockSpec((nb, 3, d), lambda b, t: (b, 0, 0)),
                  const(1, d), const(d, o1), const(d, hk), const(d, 2 * LANES),
                  const(2 * nheads, d), const(CONV_WIDTH, o1), const(1, LANES), const(1, LANES),
                  const(2 * nheads, rows), const(2 * nheads, rows), const(LANES, LANES),
                  const(LANES, LANES), const(1, hk), const(hk, d), const(1, d)],
        out_specs=pl.BlockSpec((nb, tm, d), lambda b, t: (b, t, 0)),
        out_shape=jax.ShapeDtypeStruct((bsz, s, d), F32),
        scratch_shapes=[pltpu.VMEM((nb, SUBLANES, o1), F32),
                        pltpu.VMEM((nb * npairs, 2, HEAD_DIM, HEAD_DIM), F32),
                        pair_stack, pair_stack, pair_stack,
                        pltpu.VMEM((rows, hk), F32)],
        compiler_params=pltpu.CompilerParams(dimension_semantics=("arbitrary", "arbitrary"),
                                             vmem_limit_bytes=VMEM_LIMIT_BYTES),
        name="gdn_layer",
    )(x, cond_i, norm_g.reshape(1, d), wqkv, wz, wab, wabt, conv_w, lane_vec(a_log),
      lane_vec(dt_bias), row_vec(a_log), row_vec(dt_bias), jnp.asarray(tri, BF16),
      jnp.asarray(tri.T, BF16), jnp.tile(onorm_g, nheads).reshape(1, hk), out_w.astype(BF16),
      final_g.reshape(1, d))


def kernel(x, c, ada_w, ada_b, norm_g, final_g, lru_in_w, lru_conv_w, lru_conv_b, lru_gate_w,
           lru_gate_b, lru_lambda, lru_out_w, gdn_in_w, gdn_conv_w, gdn_a_log, gdn_dt_bias,
           gdn_onorm_g, gdn_out_w):
    depth = ada_w.shape[0]
    assert depth % 2 == 0, "layers alternate RG-LRU / DeltaNet and the last one applies the final norm"
    cond = _cond(c, ada_w, ada_b)
    for i in range(depth):
        j = i // 2
        if i % 2 == 0:
            x = _lru_layer(x, cond[i], norm_g[i], lru_in_w[j], lru_conv_w[j], lru_conv_b[j],
                           lru_gate_w[j], lru_gate_b[j], lru_lambda[j], lru_out_w[j], nb=1, ns=4, tm=256)
        else:
            x = _gdn_layer(x, cond[i], norm_g[i], gdn_in_w[j], gdn_conv_w[j], gdn_a_log[j],
                           gdn_dt_bias[j], gdn_onorm_g[j], gdn_out_w[j], final_g, nb=1, tm=256,
                           final_norm=(i == depth - 1))
    return x
```

```python
import functools

import numpy as np
import jax
import jax.numpy as jnp
from jax import lax
from jax.experimental import pallas as pl
from jax.experimental.pallas import tpu as pltpu

F32 = jnp.float32
BF16 = jnp.bfloat16

EPS = 1e-6
RG_C = 8.0
CONV_WIDTH = 4
SUBLANES = 8
LANES = 128
GDN_CHUNK = 64
HEAD_DIM = 128
LRU_BLOCK_W = 128
LRU_COL_BLOCK = 256
GDN_COL_BLOCK = 256
VMEM_LIMIT_BYTES = 56 * 1024 * 1024


def _sigmoid(v):
    return jax.nn.sigmoid(v)


def _softplus(v):
    return jnp.maximum(v, 0.0) + jnp.log1p(jnp.exp(-jnp.abs(v)))


def _bdot(a, b):
    return jnp.dot(a.astype(BF16), b.astype(BF16), preferred_element_type=F32)


def _bmm(a, b):
    return lax.dot_general(a.astype(BF16), b.astype(BF16), (((2,), (1,)), ((0,), (0,))),
                           preferred_element_type=F32)


def _bmm_nt(a, b):
    return lax.dot_general(a.astype(BF16), b.astype(BF16), (((2,), (2,)), ((0,), (0,))),
                           preferred_element_type=F32)


def _bmm_tn(a, b):
    return lax.dot_general(a.astype(BF16), b.astype(BF16), (((1,), (1,)), ((0,), (0,))),
                           preferred_element_type=F32)


def _split3(v):
    v1 = v.astype(BF16)
    r1 = v - v1.astype(F32)
    v2 = r1.astype(BF16)
    v3 = (r1 - v2.astype(F32)).astype(BF16)
    return v1, v2, v3


def _modulated_rmsnorm(x, g, shift, scale):
    ms = jnp.mean(x * x, axis=-1, keepdims=True)
    return (x * lax.rsqrt(ms + EPS)) * g * (1.0 + scale) + shift


def _interleave(order, **gens):
    def step(name):
        try:
            next(gens[name])
        except StopIteration:
            gens.pop(name)

    for name in order:
        if name in gens:
            step(name)
    while gens:
        for name in list(gens):
            step(name)


def _cond_kernel(c_ref, w_ref, b_ref, o_ref):
    c = c_ref[...]
    ca = c * _sigmoid(c)
    o_ref[0] = _bdot(ca, w_ref[0]) + b_ref[0]


def _cond(c, ada_w, ada_b):
    depth, d, d3 = ada_w.shape
    bsz = c.shape[0]
    bp = -(-bsz // SUBLANES) * SUBLANES
    c8 = jnp.pad(c, ((0, bp - bsz), (0, 0)))
    tn = d
    out = pl.pallas_call(
        _cond_kernel,
        grid=(depth, d3 // tn),
        in_specs=[pl.BlockSpec((bp, d), lambda i, j: (0, 0)),
                  pl.BlockSpec((1, d, tn), lambda i, j: (i, 0, j)),
                  pl.BlockSpec((1, 1, tn), lambda i, j: (i, 0, j))],
        out_specs=pl.BlockSpec((1, bp, tn), lambda i, j: (i, 0, j)),
        out_shape=jax.ShapeDtypeStruct((depth, bp, d3), F32),
        name="adaln_cond",
    )(c8, ada_w, ada_b.reshape(depth, 1, d3))
    return out[:, :bsz].reshape(depth, bsz, 3, d)


def _lru_kernel(x_ref, cond_ref, ng_ref, perm_ref, permt_ref, inw_ref, cw_ref, cb_ref, gw_ref,
                gb_ref, lam_ref, outw_ref, o_ref, tail_ref, carry_ref, *, nb, ns, tm, d):
    t = pl.program_id(1)

    @pl.when(t == 0)
    def _():
        tail_ref[...] = jnp.zeros_like(tail_ref)
        carry_ref[...] = jnp.zeros_like(carry_ref)

    ncol = d // LRU_COL_BLOCK
    tiles = [(b, s) for s in range(ns) for b in range(nb)]

    def permuted_input(b, s):
        x = x_ref[b, pl.ds(s * tm, tm), :]
        h = _modulated_rmsnorm(x, ng_ref[...], cond_ref[b, 0:1, :], cond_ref[b, 1:2, :])
        return jnp.dot(perm_ref[...], h.astype(BF16), preferred_element_type=F32).astype(BF16)

    def in_proj(hp, n):
        cols = slice(n * LRU_COL_BLOCK, (n + 1) * LRU_COL_BLOCK)
        zcols = slice(d + n * LRU_COL_BLOCK, d + (n + 1) * LRU_COL_BLOCK)
        return (jnp.dot(hp, inw_ref[:, cols], preferred_element_type=F32),
                jnp.dot(hp, inw_ref[:, zcols], preferred_element_type=F32))

    def out_proj(tile, yn, n):
        b, s = tile
        rows = pl.ds(s * tm, tm)
        cols = slice(n * LRU_COL_BLOCK, (n + 1) * LRU_COL_BLOCK)
        out = jnp.dot(yn, outw_ref[:, cols], preferred_element_type=F32)
        o_ref[b, rows, cols] = x_ref[b, rows, cols] + cond_ref[b, 2:3, cols] * out

    def unpermute(y_blocks):
        y = jnp.concatenate(y_blocks, axis=1)
        return jnp.dot(permt_ref[...], y, preferred_element_type=F32).astype(BF16)

    hp = permuted_input(*tiles[0])
    proj = [in_proj(hp, n) for n in range(ncol)]
    prev, yn_prev = None, None
    for idx, (b, s) in enumerate(tiles):
        nxt = tiles[idx + 1] if idx + 1 < len(tiles) else None
        if nxt is not None:
            hp = permuted_input(*nxt)
        proj_next, y_blocks = [], []
        for n in range(ncol):
            if nxt is not None:
                proj_next.append(in_proj(hp, n))
            if prev is not None:
                out_proj(prev, yn_prev, n)
            cols = slice(n * LRU_COL_BLOCK, (n + 1) * LRU_COL_BLOCK)
            y_blocks.append(_lru_recurrence_block(
                proj[n][0], proj[n][1], cw_ref.at[:, cols], cb_ref.at[:, cols],
                gw_ref.at[pl.ds(n * (LRU_COL_BLOCK // LRU_BLOCK_W), LRU_COL_BLOCK // LRU_BLOCK_W)],
                gb_ref.at[:, cols], lam_ref.at[:, cols], tail_ref.at[b, :, cols],
                carry_ref.at[b, :, cols], tm=tm))
        proj = proj_next
        prev, yn_prev = (b, s), unpermute(y_blocks)
    for n in range(ncol):
        out_proj(prev, yn_prev, n)


def _lru_recurrence_block(xb, zg, cw_ref, cb_ref, gw_ref, gb_ref, lam_ref, tail_ref, carry_ref, *, tm):
    g_rows = tm // SUBLANES
    w = xb.shape[1]
    n_tail = (CONV_WIDTH - 1) * SUBLANES
    cur_tail = xb[tm - n_tail:, :]
    prev_tail = tail_ref[...]
    sub = lax.broadcasted_iota(jnp.int32, (SUBLANES, w), 0)
    heads = []
    for m in range(CONV_WIDTH - 1):
        cur = pltpu.roll(cur_tail[m * SUBLANES:(m + 1) * SUBLANES, :], 1, 0)
        prv = pltpu.roll(prev_tail[m * SUBLANES:(m + 1) * SUBLANES, :], 1, 0)
        heads.append(jnp.where(sub == 0, prv, cur))
    tail_ref[...] = cur_tail
    ext = jnp.concatenate(heads + [xb], axis=0)
    xf = cb_ref[...]
    for k in range(CONV_WIDTH):
        off = n_tail - k * SUBLANES
        xf = xf + cw_ref[CONV_WIDTH - 1 - k:CONV_WIDTH - k, :] * ext[off:off + tm, :]

    xfb = xf.astype(BF16)
    pre_r, pre_i = [], []
    for n in range(w // LRU_BLOCK_W):
        pre = jnp.dot(xfb[:, n * LRU_BLOCK_W:(n + 1) * LRU_BLOCK_W], gw_ref[n],
                      preferred_element_type=F32)
        pre_r.append(pre[:, :LRU_BLOCK_W])
        pre_i.append(pre[:, LRU_BLOCK_W:])
    r_t = _sigmoid(jnp.concatenate(pre_r, axis=1) + gb_ref[0:1, :])
    i_t = _sigmoid(jnp.concatenate(pre_i, axis=1) + gb_ref[1:2, :])
    log_a = (-RG_C * _softplus(-lam_ref[...])) * r_t
    a_t = jnp.exp(log_a)
    th = jnp.tanh(log_a)
    mult = jnp.sqrt(-2.0 * th / (1.0 - th))
    b_t = mult * (i_t * xf)

    h_loc = jnp.zeros((SUBLANES, w), F32)
    p_loc = jnp.ones((SUBLANES, w), F32)
    h_rows, p_rows = [], []
    for i in range(g_rows):
        a_i = a_t[i * SUBLANES:(i + 1) * SUBLANES, :]
        h_loc = a_i * h_loc + b_t[i * SUBLANES:(i + 1) * SUBLANES, :]
        p_loc = a_i * p_loc
        h_rows.append(h_loc)
        p_rows.append(p_loc)
    c_in = jnp.broadcast_to(carry_ref[SUBLANES - 1:SUBLANES, :], (SUBLANES, w))
    c_sub = c_in
    for _ in range(SUBLANES - 1):
        c_sub = jnp.where(sub == 0, c_in, pltpu.roll(h_loc + p_loc * c_sub, 1, 0))
    carry_ref[...] = h_loc + p_loc * c_sub
    hseq = jnp.concatenate([hr + pr * c_sub for hr, pr in zip(h_rows, p_rows)], axis=0)
    return (hseq * (zg * _sigmoid(zg))).astype(BF16)


def _time_permutation(tm):
    g_rows = tm // SUBLANES
    r = np.arange(tm)
    src = (r % SUBLANES) * g_rows + r // SUBLANES
    p = np.zeros((tm, tm), np.float32)
    p[r, src] = 1.0
    return p


def _lru_layer(x, cond_i, norm_g, in_w, conv_w, conv_b, gate_w, gate_b, lam, out_w, *, nb, ns, tm):
    bsz, s, d = x.shape
    nblk = d // LRU_BLOCK_W
    perm = _time_permutation(tm)
    gw = jnp.concatenate([gate_w[0], gate_w[1]], axis=-1).astype(BF16)
    const = lambda *shape: pl.BlockSpec(shape, lambda b, t: (0,) * len(shape))
    kern = functools.partial(_lru_kernel, nb=nb, ns=ns, tm=tm, d=d)
    return pl.pallas_call(
        kern,
        grid=(bsz // nb, s // (ns * tm)),
        in_specs=[pl.BlockSpec((nb, ns * tm, d), lambda b, t: (b, t, 0)),
                  pl.BlockSpec((nb, 3, d), lambda b, t: (b, 0, 0)),
                  const(1, d), const(tm, tm), const(tm, tm), const(d, 2 * d), const(CONV_WIDTH, d),
                  const(1, d), const(nblk, LRU_BLOCK_W, 2 * LRU_BLOCK_W), const(2, d), const(1, d),
                  const(d, d)],
        out_specs=pl.BlockSpec((nb, ns * tm, d), lambda b, t: (b, t, 0)),
        out_shape=jax.ShapeDtypeStruct((bsz, s, d), F32),
        scratch_shapes=[pltpu.VMEM((nb, (CONV_WIDTH - 1) * SUBLANES, d), F32),
                        pltpu.VMEM((nb, SUBLANES, d), F32)],
        compiler_params=pltpu.CompilerParams(dimension_semantics=("arbitrary", "arbitrary"),
                                             vmem_limit_bytes=VMEM_LIMIT_BYTES),
        name="rglru_layer",
    )(x, cond_i, norm_g.reshape(1, d), jnp.asarray(perm, BF16), jnp.asarray(perm.T, BF16),
      in_w.astype(BF16), conv_w, conv_b.reshape(1, d), gw, gate_b, lam.reshape(1, d),
      out_w.astype(BF16))


def _gdn_kernel(x_ref, cond_ref, ng_ref, wqkv_ref, wz_ref, wab_ref, wabt_ref, cw_ref, alog_ref,
                dtb_ref, alogr_ref, dtbr_ref, tri_ref, trit_ref, og_ref, outw_ref, fg_ref, o_ref,
                hist_ref, state_ref, qp_ref, kp_ref, vp_ref,
                *, nb, ns, tm, d, nheads, final_norm):
    c_len = GDN_CHUNK
    nchunks = tm // c_len
    npairs = nheads // 2
    n_mat = nchunks * npairs
    hk = nheads * HEAD_DIM
    t = pl.program_id(1)

    @pl.when(t == 0)
    def _():
        hist_ref[:, :SUBLANES, :] = jnp.zeros((nb, SUBLANES, 3 * hk), F32)
        state_ref[...] = jnp.zeros_like(state_ref)

    rr = lax.broadcasted_iota(jnp.int32, (2 * c_len, 2 * c_len), 0)
    cc = lax.broadcasted_iota(jnp.int32, (2 * c_len, 2 * c_len), 1)
    same = (rr >= c_len) == (cc >= c_len)
    causal = same & (rr >= cc)
    strict = same & (rr > cc)
    eye = (rr == cc).astype(F32)
    merge_masks = []
    blk = 1
    while blk < c_len:
        merge_masks.append(((rr ^ cc) < 2 * blk) & ((rr & blk) != 0) & ((cc & blk) == 0))
        blk *= 2

    def stage_a(tile, slot, ctx):
        b, s = tile
        rows = pl.ds(s * tm, tm)
        hb = _modulated_rmsnorm(x_ref[b, rows, :], ng_ref[...], cond_ref[b, 0:1, :],
                                cond_ref[b, 1:2, :]).astype(BF16)
        yield
        z_blocks = []
        for j in range(hk // GDN_COL_BLOCK):
            cols = slice(j * GDN_COL_BLOCK, (j + 1) * GDN_COL_BLOCK)
            z_blocks.append(jnp.dot(hb, wz_ref[:, cols], preferred_element_type=F32))
            yield
        ctx["z"] = jnp.concatenate(z_blocks, axis=1)
        heads_per_piece = GDN_COL_BLOCK // HEAD_DIM
        for j in range(3 * hk // GDN_COL_BLOCK):
            cols = slice(j * GDN_COL_BLOCK, (j + 1) * GDN_COL_BLOCK)
            pre = jnp.dot(hb, wqkv_ref[:, cols], preferred_element_type=F32)
            hist_ref[b, SUBLANES:, cols] = pre
            conv = cw_ref[CONV_WIDTH - 1:CONV_WIDTH, cols] * pre
            for k in range(1, CONV_WIDTH):
                conv = conv + (cw_ref[CONV_WIDTH - 1 - k:CONV_WIDTH - k, cols]
                               * hist_ref[b, SUBLANES - k:SUBLANES - k + tm, cols])
            hist_ref[b, :SUBLANES, cols] = pre[tm - SUBLANES:, :]
            act = conv * _sigmoid(conv)
            for i in range(heads_per_piece):
                which, h = divmod(j * heads_per_piece + i, nheads)
                a_h = act[:, i * HEAD_DIM:(i + 1) * HEAD_DIM]
                if which < 2:
                    inv = lax.rsqrt(jnp.sum(a_h * a_h, axis=-1, keepdims=True) + EPS)
                    a_h = a_h * (inv * (HEAD_DIM ** -0.5) if which == 0 else inv)
                dst_ref = (qp_ref, kp_ref, vp_ref)[which]
                hp, par = divmod(h, 2)
                for c in range(nchunks):
                    dst_ref[slot, c, hp, par * c_len:(par + 1) * c_len, :] = (
                        a_h[c * c_len:(c + 1) * c_len, :])
            yield
        ab = jnp.dot(hb, wab_ref[...], preferred_element_type=F32)
        abt = lax.dot_general(wabt_ref[...], hb, (((1,), (1,)), ((), ())),
                              preferred_element_type=F32)
        g_col = -jnp.exp(alog_ref[...]) * _softplus(ab[:, :LANES] + dtb_ref[...])
        g_row = -jnp.exp(alogr_ref[...]) * _softplus(abt + dtbr_ref[...])
        ctx["beta"] = _sigmoid(ab[:, LANES:])
        gcol_parts = _split3(g_col)
        grow_parts = _split3(g_row)
        gc_blocks, gcr_blocks = [], []
        for i in range(tm // LANES):
            sl = slice(i * LANES, (i + 1) * LANES)
            gc_blocks.append(sum(jnp.dot(tri_ref[...], p[sl, :], preferred_element_type=F32)
                                 for p in gcol_parts))
            gcr_blocks.append(sum(jnp.dot(p[:, sl], trit_ref[...], preferred_element_type=F32)
                                  for p in grow_parts))
        ctx["gc"] = jnp.concatenate(gc_blocks, axis=0)
        ctx["gcr"] = jnp.concatenate(gcr_blocks, axis=1)
        yield

    def pair_cols(arr):
        mats = []
        for c in range(nchunks):
            blk_rows = arr[c * c_len:(c + 1) * c_len, :]
            for hp in range(npairs):
                mats.append(jnp.concatenate(
                    [jnp.broadcast_to(blk_rows[:, 2 * hp + par:2 * hp + par + 1], (c_len, HEAD_DIM))
                     for par in range(2)], axis=0))
        return jnp.stack(mats)

    def pair_rows(arr):
        mats = []
        for c in range(nchunks):
            for hp in range(npairs):
                mats.append(jnp.concatenate(
                    [arr[2 * hp + par:2 * hp + par + 1, c * c_len:(c + 1) * c_len]
                     for par in range(2)], axis=1))
        return jnp.stack(mats)

    def merge_level(t_mat, a_mat, m, blk):
        n = 2 * c_len
        if blk < SUBLANES:
            return t_mat - _bmm(t_mat, _bmm(jnp.where(m[None], a_mat, 0.0), t_mat))
        ups = [slice(r, r + blk) for r in range(0, n, 2 * blk)]
        lows = [slice(r, r + blk) for r in range(blk, n, 2 * blk)]
        o_lo = jnp.concatenate([jnp.where(m[sl][None], a_mat[:, sl], 0.0) for sl in lows], axis=1)
        x_lo = _bmm(o_lo, t_mat)
        zeros = jnp.zeros((n_mat, blk, n), F32)
        x_full = jnp.concatenate(
            [piece for i in range(len(lows)) for piece in (zeros, x_lo[:, i * blk:(i + 1) * blk])],
            axis=1)
        upd = _bmm(jnp.concatenate([t_mat[:, sl] for sl in lows], axis=1), x_full)
        return jnp.concatenate(
            [piece for i, (up, lo) in enumerate(zip(ups, lows))
             for piece in (t_mat[:, up], t_mat[:, lo] - upd[:, i * blk:(i + 1) * blk])], axis=1)

    def stage_b(tile, slot, ctx):
        b, s = tile
        rows = pl.ds(s * tm, tm)
        q2 = qp_ref[slot].reshape(n_mat, 2 * c_len, HEAD_DIM)
        k2 = kp_ref[slot].reshape(n_mat, 2 * c_len, HEAD_DIM)
        v2 = vp_ref[slot].reshape(n_mat, 2 * c_len, HEAD_DIM)
        beta2 = pair_cols(ctx["beta"])
        gcol2 = pair_cols(ctx["gc"])
        grow2 = pair_rows(ctx["gcr"])
        yield
        eg2 = jnp.exp(gcol2)
        glast2 = jnp.concatenate(
            [jnp.broadcast_to(gcol2[:, (par + 1) * c_len - 1:(par + 1) * c_len, :],
                              (n_mat, c_len, HEAD_DIM)) for par in range(2)], axis=1)
        kd2 = jnp.exp(glast2 - gcol2)
        decay = jnp.exp(jnp.where(causal[None], gcol2 - grow2, -1e30))
        yield
        kb2 = k2 * beta2
        gram = _bmm_nt(jnp.concatenate([kb2, q2], axis=1), k2)
        a_mat = jnp.where(strict[None], gram[:, :2 * c_len] * decay, 0.0)
        attn = gram[:, 2 * c_len:] * decay
        yield
        t_mat = eye[None] - jnp.where(merge_masks[0][None], a_mat, 0.0)
        for lvl, m in enumerate(merge_masks[1:], start=1):
            t_mat = merge_level(t_mat, a_mat, m, 2 ** lvl)
            yield
        wu = _bmm(t_mat, jnp.concatenate([kb2 * eg2, v2 * beta2], axis=2))
        w2 = wu[:, :, :HEAD_DIM]
        u2 = wu[:, :, HEAD_DIM:]
        qd2 = q2 * eg2
        kdec2 = k2 * kd2
        yield
        st = slice(b * npairs, (b + 1) * npairs)
        o_chunks = []
        for c in range(nchunks):
            sl = slice(c * npairs, (c + 1) * npairs)
            vn_parts, qs_parts = [], []
            for par in range(2):
                rs = slice(par * c_len, (par + 1) * c_len)
                wq = _bmm(jnp.concatenate([w2[sl, rs], qd2[sl, rs]], axis=1), state_ref[st, par])
                vn_parts.append(u2[sl, rs] - wq[:, :c_len])
                qs_parts.append(wq[:, c_len:])
            v_new = jnp.concatenate(vn_parts, axis=1)
            o_chunks.append(jnp.concatenate(qs_parts, axis=1) + _bmm(attn[sl], v_new))
            for par in range(2):
                rs = slice(par * c_len, (par + 1) * c_len)
                e_last = jnp.broadcast_to(eg2[sl, (par + 1) * c_len - 1:(par + 1) * c_len, :],
                                          (npairs, HEAD_DIM, HEAD_DIM))
                state_ref[st, par] = (state_ref[st, par] * e_last
                                      + _bmm_tn(kdec2[sl, rs], v_new[:, rs]))
            yield
        parts = []
        for h in range(nheads):
            hp, par = divmod(h, 2)
            oh = jnp.concatenate([oc[hp, par * c_len:(par + 1) * c_len, :] for oc in o_chunks],
                                 axis=0)
            parts.append(oh * lax.rsqrt(jnp.mean(oh * oh, axis=-1, keepdims=True) + EPS))
        z = ctx["z"]
        og = (jnp.concatenate(parts, axis=1) * og_ref[...] * (z * _sigmoid(z))).astype(BF16)
        yield
        xn_blocks = []
        for j in range(d // GDN_COL_BLOCK):
            cols = slice(j * GDN_COL_BLOCK, (j + 1) * GDN_COL_BLOCK)
            out = jnp.dot(og, outw_ref[:, cols], preferred_element_type=F32)
            xn = x_ref[b, rows, cols] + cond_ref[b, 2:3, cols] * out
            if final_norm:
                xn_blocks.append(xn)
            else:
                o_ref[b, rows, cols] = xn
            yield
        if final_norm:
            xn = jnp.concatenate(xn_blocks, axis=1)
            ms = jnp.mean(xn * xn, axis=-1, keepdims=True)
            o_ref[b, rows, :] = xn * lax.rsqrt(ms + EPS) * fg_ref[...]

    tiles = [(b, s) for s in range(ns) for b in range(nb)]
    ctxs = [dict() for _ in tiles]
    order = "abab" + "a" + "ab" + "a" + "ab" * 10 + "aaa"
    _interleave("", a=stage_a(tiles[0], 0, ctxs[0]))
    for idx, tile in enumerate(tiles):
        gens = dict(b=stage_b(tile, idx % 2, ctxs[idx]))
        if idx + 1 < len(tiles):
            gens["a"] = stage_a(tiles[idx + 1], (idx + 1) % 2, ctxs[idx + 1])
        _interleave(order, **gens)


def _pair_tri():
    r = np.arange(LANES)
    m = (r[:, None] >= r[None, :]) & (r[:, None] // GDN_CHUNK == r[None, :] // GDN_CHUNK)
    return m.astype(np.float32)


def _gdn_layer(x, cond_i, norm_g, in_w, conv_w, a_log, dt_bias, onorm_g, out_w, final_g, *, nb, ns,
               tm, final_norm):
    bsz, s, d = x.shape
    nheads = a_log.shape[0]
    hk = nheads * HEAD_DIM
    nchunks = tm // GDN_CHUNK
    npairs = nheads // 2
    o1 = 3 * hk
    o2 = o1 + hk
    wqkv = in_w[:, :o1].astype(BF16)
    wz = in_w[:, o1:o2].astype(BF16)
    wa = in_w[:, o2:o2 + nheads]
    wb = in_w[:, o2 + nheads:]
    pad_l = lambda w: jnp.pad(w, ((0, 0), (0, LANES - nheads)))
    wab = jnp.concatenate([pad_l(wa), pad_l(wb)], axis=1).astype(BF16)
    wabt = jnp.concatenate([wa, wb], axis=1).T.astype(BF16)
    lane_vec = lambda v: jnp.pad(v, (0, LANES - nheads)).reshape(1, LANES)
    row_vec = lambda v: jnp.broadcast_to(jnp.pad(v, (0, nheads))[:, None], (2 * nheads, tm))
    tri = _pair_tri()
    const = lambda *shape: pl.BlockSpec(shape, lambda b, t: (0,) * len(shape))
    kern = functools.partial(_gdn_kernel, nb=nb, ns=ns, tm=tm, d=d, nheads=nheads,
                             final_norm=final_norm)
    pair_stack = pltpu.VMEM((2, nchunks, npairs, 2 * GDN_CHUNK, HEAD_DIM), F32)
    return pl.pallas_call(
        kern,
        grid=(bsz // nb, s // (ns * tm)),
        in_specs=[pl.BlockSpec((nb, ns * tm, d), lambda b, t: (b, t, 0)),
                  pl.BlockSpec((nb, 3, d), lambda b, t: (b, 0, 0)),
                  const(1, d), const(d, o1), const(d, hk), const(d, 2 * LANES),
                  const(2 * nheads, d), const(CONV_WIDTH, o1), const(1, LANES), const(1, LANES),
                  const(2 * nheads, tm), const(2 * nheads, tm), const(LANES, LANES),
                  const(LANES, LANES), const(1, hk), const(hk, d), const(1, d)],
        out_specs=pl.BlockSpec((nb, ns * tm, d), lambda b, t: (b, t, 0)),
        out_shape=jax.ShapeDtypeStruct((bsz, s, d), F32),
        scratch_shapes=[pltpu.VMEM((nb, SUBLANES + tm, o1), F32),
                        pltpu.VMEM((nb * npairs, 2, HEAD_DIM, HEAD_DIM), F32),
                        pair_stack, pair_stack, pair_stack],
        compiler_params=pltpu.CompilerParams(dimension_semantics=("arbitrary", "arbitrary"),
                                             vmem_limit_bytes=VMEM_LIMIT_BYTES),
        name="gdn_layer",
    )(x, cond_i, norm_g.reshape(1, d), wqkv, wz, wab, wabt, conv_w, lane_vec(a_log),
      lane_vec(dt_bias), row_vec(a_log), row_vec(dt_bias), jnp.asarray(tri, BF16),
      jnp.asarray(tri.T, BF16), jnp.tile(onorm_g, nheads).reshape(1, hk), out_w.astype(BF16),
      final_g.reshape(1, d))


def kernel(x, c, ada_w, ada_b, norm_g, final_g, lru_in_w, lru_conv_w, lru_conv_b, lru_gate_w,
           lru_gate_b, lru_lambda, lru_out_w, gdn_in_w, gdn_conv_w, gdn_a_log, gdn_dt_bias,
           gdn_onorm_g, gdn_out_w):
    depth = ada_w.shape[0]
    assert depth % 2 == 0, "layers alternate RG-LRU / DeltaNet and the last one applies the final norm"
    cond = _cond(c, ada_w, ada_b)
    for i in range(depth):
        j = i // 2
        if i % 2 == 0:
            x = _lru_layer(x, cond[i], norm_g[i], lru_in_w[j], lru_conv_w[j], lru_conv_b[j],
                           lru_gate_w[j], lru_gate_b[j], lru_lambda[j], lru_out_w[j], nb=1, ns=4, tm=256)
        else:
            x = _gdn_layer(x, cond[i], norm_g[i], gdn_in_w[j], gdn_conv_w[j], gdn_a_log[j],
                           gdn_dt_bias[j], gdn_onorm_g[j], gdn_out_w[j], final_g, nb=1, ns=2, tm=256,
                           final_norm=(i == depth - 1))
    return x
```

```python
import functools

import numpy as np
import jax
import jax.numpy as jnp
from jax import lax
from jax.experimental import pallas as pl
from jax.experimental.pallas import tpu as pltpu

F32 = jnp.float32
BF16 = jnp.bfloat16

EPS = 1e-6
RG_C = 8.0
CONV_WIDTH = 4
SUBLANES = 8
LANES = 128
GDN_CHUNK = 64
HEAD_DIM = 128
LRU_BLOCK_W = 128
LRU_COL_BLOCK = 256
GDN_COL_BLOCK = 256
VMEM_LIMIT_BYTES = 56 * 1024 * 1024


def _sigmoid(v):
    return 0.5 * jnp.tanh(0.5 * v) + 0.5


def _softplus(v):
    return jnp.maximum(v, 0.0) + jnp.log1p(jnp.exp(-jnp.abs(v)))


def _bdot(a, b):
    return jnp.dot(a.astype(BF16), b.astype(BF16), preferred_element_type=F32)


def _bmm(a, b):
    return lax.dot_general(a.astype(BF16), b.astype(BF16), (((2,), (1,)), ((0,), (0,))),
                           preferred_element_type=F32)


def _bmm_nt(a, b):
    return lax.dot_general(a.astype(BF16), b.astype(BF16), (((2,), (2,)), ((0,), (0,))),
                           preferred_element_type=F32)


def _bmm_tn(a, b):
    return lax.dot_general(a.astype(BF16), b.astype(BF16), (((1,), (1,)), ((0,), (0,))),
                           preferred_element_type=F32)


def _split3(v):
    v1 = v.astype(BF16)
    r1 = v - v1.astype(F32)
    v2 = r1.astype(BF16)
    v3 = (r1 - v2.astype(F32)).astype(BF16)
    return v1, v2, v3


def _modulated_rmsnorm(x, g, shift, scale):
    ms = jnp.mean(x * x, axis=-1, keepdims=True)
    return (x * lax.rsqrt(ms + EPS)) * (g * (1.0 + scale)) + shift


def _interleave(order, **gens):
    def step(name):
        try:
            next(gens[name])
        except StopIteration:
            gens.pop(name)

    for name in order:
        if name in gens:
            step(name)
    while gens:
        for name in list(gens):
            step(name)


def _cond_kernel(c_ref, w_ref, b_ref, o_ref):
    c = c_ref[...]
    ca = c * _sigmoid(c)
    o_ref[0] = _bdot(ca, w_ref[0]) + b_ref[0]


def _cond(c, ada_w, ada_b):
    depth, d, d3 = ada_w.shape
    bsz = c.shape[0]
    bp = -(-bsz // SUBLANES) * SUBLANES
    c8 = jnp.pad(c, ((0, bp - bsz), (0, 0)))
    tn = d
    out = pl.pallas_call(
        _cond_kernel,
        grid=(depth, d3 // tn),
        in_specs=[pl.BlockSpec((bp, d), lambda i, j: (0, 0)),
                  pl.BlockSpec((1, d, tn), lambda i, j: (i, 0, j)),
                  pl.BlockSpec((1, 1, tn), lambda i, j: (i, 0, j))],
        out_specs=pl.BlockSpec((1, bp, tn), lambda i, j: (i, 0, j)),
        out_shape=jax.ShapeDtypeStruct((depth, bp, d3), F32),
        name="adaln_cond",
    )(c8, ada_w, ada_b.reshape(depth, 1, d3))
    return out[:, :bsz].reshape(depth, bsz, 3, d)


def _lru_kernel(x_ref, cond_ref, ng_ref, perm_ref, permt_ref, inw_ref, cw_ref, cb_ref, gw_ref,
                gb_ref, lam_ref, outw_ref, o_ref, tail_ref, carry_ref, *, nb, ns, tm, d):
    t = pl.program_id(1)

    @pl.when(t == 0)
    def _():
        tail_ref[...] = jnp.zeros_like(tail_ref)
        carry_ref[...] = jnp.zeros_like(carry_ref)

    ncol = d // LRU_COL_BLOCK
    tiles = [(b, s) for s in range(ns) for b in range(nb)]

    def permuted_input(b, s):
        x = x_ref[b, pl.ds(s * tm, tm), :]
        h = _modulated_rmsnorm(x, ng_ref[...], cond_ref[b, 0:1, :], cond_ref[b, 1:2, :])
        return jnp.dot(perm_ref[...], h.astype(BF16), preferred_element_type=F32).astype(BF16)

    def in_proj(hp, n):
        cols = slice(n * LRU_COL_BLOCK, (n + 1) * LRU_COL_BLOCK)
        zcols = slice(d + n * LRU_COL_BLOCK, d + (n + 1) * LRU_COL_BLOCK)
        return (jnp.dot(hp, inw_ref[:, cols], preferred_element_type=F32),
                jnp.dot(hp, inw_ref[:, zcols], preferred_element_type=F32))

    def out_proj(tile, yn, n):
        b, s = tile
        rows = pl.ds(s * tm, tm)
        cols = slice(n * LRU_COL_BLOCK, (n + 1) * LRU_COL_BLOCK)
        out = jnp.dot(yn, outw_ref[:, cols], preferred_element_type=F32)
        o_ref[b, rows, cols] = x_ref[b, rows, cols] + cond_ref[b, 2:3, cols] * out

    def unpermute(y_blocks):
        y = jnp.concatenate(y_blocks, axis=1)
        return jnp.dot(permt_ref[...], y, preferred_element_type=F32).astype(BF16)

    hp = permuted_input(*tiles[0])
    proj = [in_proj(hp, n) for n in range(ncol)]
    prev, yn_prev = None, None
    for idx, (b, s) in enumerate(tiles):
        nxt = tiles[idx + 1] if idx + 1 < len(tiles) else None
        if nxt is not None:
            hp = permuted_input(*nxt)
        proj_next, y_blocks = [], []
        def recurrence(n, half):
            lo = half * LRU_BLOCK_W
            cols = slice(n * LRU_COL_BLOCK + lo, n * LRU_COL_BLOCK + lo + LRU_BLOCK_W)
            blk = n * (LRU_COL_BLOCK // LRU_BLOCK_W) + half
            y_blocks.append(_lru_recurrence_block(
                proj[n][0][:, lo:lo + LRU_BLOCK_W], proj[n][1][:, lo:lo + LRU_BLOCK_W],
                cw_ref.at[:, cols], cb_ref.at[:, cols], gw_ref.at[pl.ds(blk, 1)],
                gb_ref.at[:, cols], lam_ref.at[:, cols], tail_ref.at[b, :, cols],
                carry_ref.at[b, :, cols], tm=tm))

        for n in range(ncol):
            if nxt is not None:
                proj_next.append(in_proj(hp, n))
            recurrence(n, 0)
            if prev is not None:
                out_proj(prev, yn_prev, n)
            recurrence(n, 1)
        proj = proj_next
        prev, yn_prev = (b, s), unpermute(y_blocks)
    for n in range(ncol):
        out_proj(prev, yn_prev, n)


def _lru_recurrence_block(xb, zg, cw_ref, cb_ref, gw_ref, gb_ref, lam_ref, tail_ref, carry_ref, *, tm):
    g_rows = tm // SUBLANES
    w = xb.shape[1]
    n_tail = (CONV_WIDTH - 1) * SUBLANES
    cur_tail = xb[tm - n_tail:, :]
    prev_tail = tail_ref[...]
    sub = lax.broadcasted_iota(jnp.int32, (SUBLANES, w), 0)
    heads = []
    for m in range(CONV_WIDTH - 1):
        cur = pltpu.roll(cur_tail[m * SUBLANES:(m + 1) * SUBLANES, :], 1, 0)
        prv = pltpu.roll(prev_tail[m * SUBLANES:(m + 1) * SUBLANES, :], 1, 0)
        heads.append(jnp.where(sub == 0, prv, cur))
    tail_ref[...] = cur_tail
    ext = jnp.concatenate(heads + [xb], axis=0)
    xf = cb_ref[...]
    for k in range(CONV_WIDTH):
        off = n_tail - k * SUBLANES
        xf = xf + cw_ref[CONV_WIDTH - 1 - k:CONV_WIDTH - k, :] * ext[off:off + tm, :]

    xfb = xf.astype(BF16)
    pre_r, pre_i = [], []
    for n in range(w // LRU_BLOCK_W):
        pre = jnp.dot(xfb[:, n * LRU_BLOCK_W:(n + 1) * LRU_BLOCK_W], gw_ref[n],
                      preferred_element_type=F32)
        pre_r.append(pre[:, :LRU_BLOCK_W])
        pre_i.append(pre[:, LRU_BLOCK_W:])
    r_t = _sigmoid(jnp.concatenate(pre_r, axis=1) + gb_ref[0:1, :])
    i_t = _sigmoid(jnp.concatenate(pre_i, axis=1) + gb_ref[1:2, :])
    log_a = (-RG_C * _softplus(-lam_ref[...])) * r_t
    a_t = jnp.exp(log_a)
    th = jnp.tanh(log_a)
    mult = jnp.sqrt(-2.0 * th / (1.0 - th))
    b_t = mult * (i_t * xf)

    h_loc = jnp.zeros((SUBLANES, w), F32)
    p_loc = jnp.ones((SUBLANES, w), F32)
    h_rows, p_rows = [], []
    for i in range(g_rows):
        a_i = a_t[i * SUBLANES:(i + 1) * SUBLANES, :]
        h_loc = a_i * h_loc + b_t[i * SUBLANES:(i + 1) * SUBLANES, :]
        p_loc = a_i * p_loc
        h_rows.append(h_loc)
        p_rows.append(p_loc)
    c_in = jnp.broadcast_to(carry_ref[SUBLANES - 1:SUBLANES, :], (SUBLANES, w))
    c_sub = c_in
    for _ in range(SUBLANES - 1):
        c_sub = jnp.where(sub == 0, c_in, pltpu.roll(h_loc + p_loc * c_sub, 1, 0))
    carry_ref[...] = h_loc + p_loc * c_sub
    hseq = jnp.concatenate([hr + pr * c_sub for hr, pr in zip(h_rows, p_rows)], axis=0)
    return (hseq * (zg * _sigmoid(zg))).astype(BF16)


def _time_permutation(tm):
    g_rows = tm // SUBLANES
    r = np.arange(tm)
    src = (r % SUBLANES) * g_rows + r // SUBLANES
    p = np.zeros((tm, tm), np.float32)
    p[r, src] = 1.0
    return p


def _lru_layer(x, cond_i, norm_g, in_w_all, conv_w, conv_b, gw_all, gate_b, lam, out_w_all, j, *, nb,
               ns, tm):
    bsz, s, d = x.shape
    nblk = d // LRU_BLOCK_W
    perm = _time_permutation(tm)
    const = lambda *shape: pl.BlockSpec(shape, lambda b, t: (0,) * len(shape))
    layer = lambda *shape: pl.BlockSpec((None,) + shape, lambda b, t: (j,) + (0,) * len(shape))
    kern = functools.partial(_lru_kernel, nb=nb, ns=ns, tm=tm, d=d)
    return pl.pallas_call(
        kern,
        grid=(bsz // nb, s // (ns * tm)),
        in_specs=[pl.BlockSpec((nb, ns * tm, d), lambda b, t: (b, t, 0)),
                  pl.BlockSpec((nb, 3, d), lambda b, t: (b, 0, 0)),
                  const(1, d), const(tm, tm), const(tm, tm), layer(d, 2 * d), const(CONV_WIDTH, d),
                  const(1, d), layer(nblk, LRU_BLOCK_W, 2 * LRU_BLOCK_W), const(2, d), const(1, d),
                  layer(d, d)],
        out_specs=pl.BlockSpec((nb, ns * tm, d), lambda b, t: (b, t, 0)),
        out_shape=jax.ShapeDtypeStruct((bsz, s, d), F32),
        scratch_shapes=[pltpu.VMEM((nb, (CONV_WIDTH - 1) * SUBLANES, d), F32),
                        pltpu.VMEM((nb, SUBLANES, d), F32)],
        compiler_params=pltpu.CompilerParams(dimension_semantics=("arbitrary", "arbitrary"),
                                             vmem_limit_bytes=VMEM_LIMIT_BYTES),
        name="rglru_layer",
    )(x, cond_i, norm_g.reshape(1, d), jnp.asarray(perm, BF16), jnp.asarray(perm.T, BF16),
      in_w_all, conv_w, conv_b.reshape(1, d), gw_all, gate_b, lam.reshape(1, d), out_w_all)


def _gdn_kernel(x_ref, cond_ref, ng_ref, wqkvz_ref, wab_ref, wabt_ref, cw_ref, alog_ref,
                dtb_ref, alogr_ref, dtbr_ref, tri_ref, trit_ref, og_ref, outw_ref, fg_ref, o_ref,
                hist_ref, state_ref, qp_ref, kp_ref, vp_ref,
                *, nb, ns, tm, d, nheads, final_norm):
    c_len = GDN_CHUNK
    nchunks = tm // c_len
    npairs = nheads // 2
    n_mat = nchunks * npairs
    hk = nheads * HEAD_DIM
    t = pl.program_id(1)

    @pl.when(t == 0)
    def _():
        hist_ref[:, :SUBLANES, :] = jnp.zeros((nb, SUBLANES, 3 * hk), F32)
        state_ref[...] = jnp.zeros_like(state_ref)

    rr = lax.broadcasted_iota(jnp.int32, (2 * c_len, 2 * c_len), 0)
    cc = lax.broadcasted_iota(jnp.int32, (2 * c_len, 2 * c_len), 1)
    same = (rr >= c_len) == (cc >= c_len)
    causal = same & (rr >= cc)
    strict = same & (rr > cc)
    eye = (rr == cc).astype(F32)
    merge_masks = []
    blk = 1
    while blk < c_len:
        merge_masks.append(((rr ^ cc) < 2 * blk) & ((rr & blk) != 0) & ((cc & blk) == 0))
        blk *= 2

    def stage_a(tile, slot, ctx):
        b, s = tile
        rows = pl.ds(s * tm, tm)
        hb = _modulated_rmsnorm(x_ref[b, rows, :], ng_ref[...], cond_ref[b, 0:1, :],
                                cond_ref[b, 1:2, :]).astype(BF16)
        yield
        z_blocks = []
        for j in range(hk // GDN_COL_BLOCK):
            zcols = slice(3 * hk + j * GDN_COL_BLOCK, 3 * hk + (j + 1) * GDN_COL_BLOCK)
            z_blocks.append(jnp.dot(hb, wqkvz_ref[:, zcols], preferred_element_type=F32))
            yield
        ctx["z"] = jnp.concatenate(z_blocks, axis=1)
        heads_per_piece = GDN_COL_BLOCK // HEAD_DIM
        for j in range(3 * hk // GDN_COL_BLOCK):
            cols = slice(j * GDN_COL_BLOCK, (j + 1) * GDN_COL_BLOCK)
            pre = jnp.dot(hb, wqkvz_ref[:, cols], preferred_element_type=F32)
            hist_ref[b, SUBLANES:, cols] = pre
            conv = cw_ref[CONV_WIDTH - 1:CONV_WIDTH, cols] * pre
            for k in range(1, CONV_WIDTH):
                conv = conv + (cw_ref[CONV_WIDTH - 1 - k:CONV_WIDTH - k, cols]
                               * hist_ref[b, SUBLANES - k:SUBLANES - k + tm, cols])
            hist_ref[b, :SUBLANES, cols] = pre[tm - SUBLANES:, :]
            act = conv * _sigmoid(conv)
            for i in range(heads_per_piece):
                which, h = divmod(j * heads_per_piece + i, nheads)
                a_h = act[:, i * HEAD_DIM:(i + 1) * HEAD_DIM]
                if which < 2:
                    inv = lax.rsqrt(jnp.sum(a_h * a_h, axis=-1, keepdims=True) + EPS)
                    a_h = a_h * (inv * (HEAD_DIM ** -0.5) if which == 0 else inv)
                dst_ref = (qp_ref, kp_ref, vp_ref)[which]
                hp, par = divmod(h, 2)
                for c in range(nchunks):
                    dst_ref[slot, c, hp, par * c_len:(par + 1) * c_len, :] = (
                        a_h[c * c_len:(c + 1) * c_len, :])
            yield
        ab = jnp.dot(hb, wab_ref[...], preferred_element_type=F32)
        abt = lax.dot_general(wabt_ref[...], hb, (((1,), (1,)), ((), ())),
                              preferred_element_type=F32)
        g_col = -jnp.exp(alog_ref[...]) * _softplus(ab[:, :LANES] + dtb_ref[...])
        g_row = -jnp.exp(alogr_ref[...]) * _softplus(abt + dtbr_ref[...])
        ctx["beta"] = _sigmoid(ab[:, LANES:])
        gcol_parts = _split3(g_col)
        grow_parts = _split3(g_row)
        gc_blocks, gcr_blocks = [], []
        for i in range(tm // LANES):
            sl = slice(i * LANES, (i + 1) * LANES)
            gc_blocks.append(sum(jnp.dot(tri_ref[...], p[sl, :], preferred_element_type=F32)
                                 for p in gcol_parts))
            gcr_blocks.append(sum(jnp.dot(p[:, sl], trit_ref[...], preferred_element_type=F32)
                                  for p in grow_parts))
        ctx["gc"] = jnp.concatenate(gc_blocks, axis=0)
        ctx["gcr"] = jnp.concatenate(gcr_blocks, axis=1)
        yield

    def pair_cols(arr):
        mats = []
        for c in range(nchunks):
            blk_rows = arr[c * c_len:(c + 1) * c_len, :]
            for hp in range(npairs):
                mats.append(jnp.concatenate(
                    [jnp.broadcast_to(blk_rows[:, 2 * hp + par:2 * hp + par + 1], (c_len, HEAD_DIM))
                     for par in range(2)], axis=0))
        return jnp.stack(mats)

    def pair_rows(arr):
        mats = []
        for c in range(nchunks):
            for hp in range(npairs):
                mats.append(jnp.concatenate(
                    [arr[2 * hp + par:2 * hp + par + 1, c * c_len:(c + 1) * c_len]
                     for par in range(2)], axis=1))
        return jnp.stack(mats)

    def merge_level(t_mat, a_mat, m, blk):
        n = 2 * c_len
        if blk < SUBLANES:
            return t_mat - _bmm(t_mat, _bmm(jnp.where(m[None], a_mat, 0.0), t_mat))
        ups = [slice(r, r + blk) for r in range(0, n, 2 * blk)]
        lows = [slice(r, r + blk) for r in range(blk, n, 2 * blk)]
        o_lo = jnp.concatenate([jnp.where(m[sl][None], a_mat[:, sl], 0.0) for sl in lows], axis=1)
        x_lo = _bmm(o_lo, t_mat)
        zeros = jnp.zeros((n_mat, blk, n), F32)
        x_full = jnp.concatenate(
            [piece for i in range(len(lows)) for piece in (zeros, x_lo[:, i * blk:(i + 1) * blk])],
            axis=1)
        upd = _bmm(jnp.concatenate([t_mat[:, sl] for sl in lows], axis=1), x_full)
        return jnp.concatenate(
            [piece for i, (up, lo) in enumerate(zip(ups, lows))
             for piece in (t_mat[:, up], t_mat[:, lo] - upd[:, i * blk:(i + 1) * blk])], axis=1)

    def stage_b(tile, slot, ctx):
        b, s = tile
        rows = pl.ds(s * tm, tm)
        q2 = qp_ref[slot].reshape(n_mat, 2 * c_len, HEAD_DIM)
        k2 = kp_ref[slot].reshape(n_mat, 2 * c_len, HEAD_DIM)
        v2 = vp_ref[slot].reshape(n_mat, 2 * c_len, HEAD_DIM)
        beta2 = pair_cols(ctx["beta"])
        gcol2 = pair_cols(ctx["gc"])
        grow2 = pair_rows(ctx["gcr"])
        yield
        eg2 = jnp.exp(gcol2)
        glast2 = jnp.concatenate(
            [jnp.broadcast_to(gcol2[:, (par + 1) * c_len - 1:(par + 1) * c_len, :],
                              (n_mat, c_len, HEAD_DIM)) for par in range(2)], axis=1)
        kd2 = jnp.exp(glast2 - gcol2)
        decay = jnp.exp(jnp.where(causal[None], gcol2 - grow2, -1e30))
        yield
        kb2 = k2 * beta2
        gram = _bmm_nt(jnp.concatenate([kb2, q2], axis=1), k2)
        a_mat = jnp.where(strict[None], gram[:, :2 * c_len] * decay, 0.0)
        attn = gram[:, 2 * c_len:] * decay
        yield
        t_mat = eye[None] - jnp.where(merge_masks[0][None], a_mat, 0.0)
        for lvl, m in enumerate(merge_masks[1:], start=1):
            t_mat = merge_level(t_mat, a_mat, m, 2 ** lvl)
            yield
        wu = _bmm(t_mat, jnp.concatenate([kb2 * eg2, v2 * beta2], axis=2))
        w2 = wu[:, :, :HEAD_DIM]
        u2 = wu[:, :, HEAD_DIM:]
        qd2 = q2 * eg2
        kdec2 = k2 * kd2
        yield
        st = slice(b * npairs, (b + 1) * npairs)
        o_chunks = []
        for c in range(nchunks):
            sl = slice(c * npairs, (c + 1) * npairs)
            vn_parts, qs_parts = [], []
            for par in range(2):
                rs = slice(par * c_len, (par + 1) * c_len)
                wq = _bmm(jnp.concatenate([w2[sl, rs], qd2[sl, rs]], axis=1), state_ref[st, par])
                vn_parts.append(u2[sl, rs] - wq[:, :c_len])
                qs_parts.append(wq[:, c_len:])
            v_new = jnp.concatenate(vn_parts, axis=1)
            o_chunks.append(jnp.concatenate(qs_parts, axis=1) + _bmm(attn[sl], v_new))
            for par in range(2):
                rs = slice(par * c_len, (par + 1) * c_len)
                e_last = jnp.broadcast_to(eg2[sl, (par + 1) * c_len - 1:(par + 1) * c_len, :],
                                          (npairs, HEAD_DIM, HEAD_DIM))
                state_ref[st, par] = (state_ref[st, par] * e_last
                                      + _bmm_tn(kdec2[sl, rs], v_new[:, rs]))
            yield
        parts = []
        for h in range(nheads):
            hp, par = divmod(h, 2)
            oh = jnp.concatenate([oc[hp, par * c_len:(par + 1) * c_len, :] for oc in o_chunks],
                                 axis=0)
            parts.append(oh * lax.rsqrt(jnp.mean(oh * oh, axis=-1, keepdims=True) + EPS))
        z = ctx["z"]
        og = (jnp.concatenate(parts, axis=1) * og_ref[...] * (z * _sigmoid(z))).astype(BF16)
        yield
        xn_blocks = []
        for j in range(d // GDN_COL_BLOCK):
            cols = slice(j * GDN_COL_BLOCK, (j + 1) * GDN_COL_BLOCK)
            out = jnp.dot(og, outw_ref[:, cols], preferred_element_type=F32)
            xn = x_ref[b, rows, cols] + cond_ref[b, 2:3, cols] * out
            if final_norm:
                xn_blocks.append(xn)
            else:
                o_ref[b, rows, cols] = xn
            yield
        if final_norm:
            xn = jnp.concatenate(xn_blocks, axis=1)
            ms = jnp.mean(xn * xn, axis=-1, keepdims=True)
            o_ref[b, rows, :] = xn * lax.rsqrt(ms + EPS) * fg_ref[...]

    tiles = [(b, s) for s in range(ns) for b in range(nb)]
    ctxs = [dict() for _ in tiles]
    order = "abab" + "a" + "ab" + "a" + "ab" * 10 + "aaa"
    _interleave("", a=stage_a(tiles[0], 0, ctxs[0]))
    for idx, tile in enumerate(tiles):
        gens = dict(b=stage_b(tile, idx % 2, ctxs[idx]))
        if idx + 1 < len(tiles):
            gens["a"] = stage_a(tiles[idx + 1], (idx + 1) % 2, ctxs[idx + 1])
        _interleave(order, **gens)


def _pair_tri():
    r = np.arange(LANES)
    m = (r[:, None] >= r[None, :]) & (r[:, None] // GDN_CHUNK == r[None, :] // GDN_CHUNK)
    return m.astype(np.float32)


def _gdn_layer(x, cond_i, norm_g, wqkvz_all, w_ab, conv_w, a_log, dt_bias, onorm_g, out_w_all,
               final_g, j, *, nb, ns, tm, final_norm):
    bsz, s, d = x.shape
    nheads = a_log.shape[0]
    hk = nheads * HEAD_DIM
    nchunks = tm // GDN_CHUNK
    npairs = nheads // 2
    o1 = 3 * hk
    wa = w_ab[:, :nheads]
    wb = w_ab[:, nheads:]
    pad_l = lambda w: jnp.pad(w, ((0, 0), (0, LANES - nheads)))
    wab = jnp.concatenate([pad_l(wa), pad_l(wb)], axis=1).astype(BF16)
    wabt = w_ab.T.astype(BF16)
    lane_vec = lambda v: jnp.pad(v, (0, LANES - nheads)).reshape(1, LANES)
    row_vec = lambda v: jnp.broadcast_to(jnp.pad(v, (0, nheads))[:, None], (2 * nheads, tm))
    tri = _pair_tri()
    const = lambda *shape: pl.BlockSpec(shape, lambda b, t: (0,) * len(shape))
    layer = lambda *shape: pl.BlockSpec((None,) + shape, lambda b, t: (j,) + (0,) * len(shape))
    kern = functools.partial(_gdn_kernel, nb=nb, ns=ns, tm=tm, d=d, nheads=nheads,
                             final_norm=final_norm)
    pair_stack = pltpu.VMEM((2, nchunks, npairs, 2 * GDN_CHUNK, HEAD_DIM), F32)
    return pl.pallas_call(
        kern,
        grid=(bsz // nb, s // (ns * tm)),
        in_specs=[pl.BlockSpec((nb, ns * tm, d), lambda b, t: (b, t, 0)),
                  pl.BlockSpec((nb, 3, d), lambda b, t: (b, 0, 0)),
                  const(1, d), layer(d, o1 + hk), const(d, 2 * LANES),
                  const(2 * nheads, d), const(CONV_WIDTH, o1), const(1, LANES), const(1, LANES),
                  const(2 * nheads, tm), const(2 * nheads, tm), const(LANES, LANES),
                  const(LANES, LANES), const(1, hk), layer(hk, d), const(1, d)],
        out_specs=pl.BlockSpec((nb, ns * tm, d), lambda b, t: (b, t, 0)),
        out_shape=jax.ShapeDtypeStruct((bsz, s, d), F32),
        scratch_shapes=[pltpu.VMEM((nb, SUBLANES + tm, o1), F32),
                        pltpu.VMEM((nb * npairs, 2, HEAD_DIM, HEAD_DIM), F32),
                        pair_stack, pair_stack, pair_stack],
        compiler_params=pltpu.CompilerParams(dimension_semantics=("arbitrary", "arbitrary"),
                                             vmem_limit_bytes=VMEM_LIMIT_BYTES),
        name="gdn_layer",
    )(x, cond_i, norm_g.reshape(1, d), wqkvz_all, wab, wabt, conv_w, lane_vec(a_log),
      lane_vec(dt_bias), row_vec(a_log), row_vec(dt_bias), jnp.asarray(tri, BF16),
      jnp.asarray(tri.T, BF16), jnp.tile(onorm_g, nheads).reshape(1, hk), out_w_all,
      final_g.reshape(1, d))


def kernel(x, c, ada_w, ada_b, norm_g, final_g, lru_in_w, lru_conv_w, lru_conv_b, lru_gate_w,
           lru_gate_b, lru_lambda, lru_out_w, gdn_in_w, gdn_conv_w, gdn_a_log, gdn_dt_bias,
           gdn_onorm_g, gdn_out_w):
    depth = ada_w.shape[0]
    assert depth % 2 == 0, "layers alternate RG-LRU / DeltaNet and the last one applies the final norm"
    cond = _cond(c, ada_w, ada_b)
    lru_in_bf = lru_in_w.astype(BF16)
    lru_out_bf = lru_out_w.astype(BF16)
    lru_gw_bf = jnp.concatenate([lru_gate_w[:, 0], lru_gate_w[:, 1]], axis=-1).astype(BF16)
    n_qkvz = 4 * gdn_a_log.shape[1] * HEAD_DIM
    gdn_qkvz_bf = gdn_in_w[:, :, :n_qkvz].astype(BF16)
    gdn_out_bf = gdn_out_w.astype(BF16)
    for i in range(depth):
        j = i // 2
        if i % 2 == 0:
            x = _lru_layer(x, cond[i], norm_g[i], lru_in_bf, lru_conv_w[j], lru_conv_b[j], lru_gw_bf,
                           lru_gate_b[j], lru_lambda[j], lru_out_bf, j, nb=1, ns=4, tm=256)
        else:
            x = _gdn_layer(x, cond[i], norm_g[i], gdn_qkvz_bf, gdn_in_w[j, :, n_qkvz:], gdn_conv_w[j],
                           gdn_a_log[j], gdn_dt_bias[j], gdn_onorm_g[j], gdn_out_bf, final_g, j,
                           nb=1, ns=2, tm=256, final_norm=(i == depth - 1))
    return x
```

```python
import functools

import numpy as np
import jax
import jax.numpy as jnp
from jax import lax
from jax.experimental import pallas as pl
from jax.experimental.pallas import tpu as pltpu

F32 = jnp.float32
BF16 = jnp.bfloat16

EPS = 1e-6
RG_C = 8.0
CONV_WIDTH = 4
SUBLANES = 8
LANES = 128
GDN_CHUNK = 64
HEAD_DIM = 128
LRU_BLOCK_W = 128
LRU_COL_BLOCK = 256
GDN_COL_BLOCK = 256
VMEM_LIMIT_BYTES = 56 * 1024 * 1024


def _sigmoid(v):
    return 0.5 * jnp.tanh(0.5 * v) + 0.5


def _softplus(v):
    return jnp.maximum(v, 0.0) + jnp.log1p(jnp.exp(-jnp.abs(v)))


def _bdot(a, b):
    return jnp.dot(a.astype(BF16), b.astype(BF16), preferred_element_type=F32)


def _bmm(a, b):
    return lax.dot_general(a.astype(BF16), b.astype(BF16), (((2,), (1,)), ((0,), (0,))),
                           preferred_element_type=F32)


def _bmm_nt(a, b):
    return lax.dot_general(a.astype(BF16), b.astype(BF16), (((2,), (2,)), ((0,), (0,))),
                           preferred_element_type=F32)


def _bmm_tn(a, b):
    return lax.dot_general(a.astype(BF16), b.astype(BF16), (((1,), (1,)), ((0,), (0,))),
                           preferred_element_type=F32)


def _split3(v):
    v1 = v.astype(BF16)
    r1 = v - v1.astype(F32)
    v2 = r1.astype(BF16)
    v3 = (r1 - v2.astype(F32)).astype(BF16)
    return v1, v2, v3


def _modulated_rmsnorm(x, g, shift, scale):
    ms = jnp.mean(x * x, axis=-1, keepdims=True)
    return (x * lax.rsqrt(ms + EPS)) * (g * (1.0 + scale)) + shift


def _interleave(order, **gens):
    def step(name):
        try:
            next(gens[name])
        except StopIteration:
            gens.pop(name)

    for name in order:
        if name in gens:
            step(name)
    while gens:
        for name in list(gens):
            step(name)


def _cond_kernel(c_ref, w_ref, b_ref, o_ref):
    c = c_ref[...]
    ca = c * _sigmoid(c)
    o_ref[0] = _bdot(ca, w_ref[0]) + b_ref[0]


def _cond(c, ada_w, ada_b):
    depth, d, d3 = ada_w.shape
    bsz = c.shape[0]
    bp = -(-bsz // SUBLANES) * SUBLANES
    c8 = jnp.pad(c, ((0, bp - bsz), (0, 0)))
    tn = d
    out = pl.pallas_call(
        _cond_kernel,
        grid=(depth, d3 // tn),
        in_specs=[pl.BlockSpec((bp, d), lambda i, j: (0, 0)),
                  pl.BlockSpec((1, d, tn), lambda i, j: (i, 0, j)),
                  pl.BlockSpec((1, 1, tn), lambda i, j: (i, 0, j))],
        out_specs=pl.BlockSpec((1, bp, tn), lambda i, j: (i, 0, j)),
        out_shape=jax.ShapeDtypeStruct((depth, bp, d3), F32),
        name="adaln_cond",
    )(c8, ada_w, ada_b.reshape(depth, 1, d3))
    return out[:, :bsz].reshape(depth, bsz, 3, d)


def _lru_kernel(x_ref, cond_ref, ng_ref, perm_ref, permt_ref, inw_ref, cw_ref, cb_ref, gw_ref,
                gb_ref, lam_ref, outw_ref, o_ref, tail_ref, carry_ref, *, nb, ns, tm, d):
    t = pl.program_id(1)

    @pl.when(t == 0)
    def _():
        tail_ref[...] = jnp.zeros_like(tail_ref)
        carry_ref[...] = jnp.zeros_like(carry_ref)

    ncol = d // LRU_COL_BLOCK
    tiles = [(b, s) for s in range(ns) for b in range(nb)]

    def permuted_input(b, s):
        x = x_ref[b, pl.ds(s * tm, tm), :]
        h = _modulated_rmsnorm(x, ng_ref[...], cond_ref[b, 0:1, :], cond_ref[b, 1:2, :])
        return jnp.dot(perm_ref[...], h.astype(BF16), preferred_element_type=F32).astype(BF16)

    def in_proj(hp, n):
        cols = slice(n * LRU_COL_BLOCK, (n + 1) * LRU_COL_BLOCK)
        zcols = slice(d + n * LRU_COL_BLOCK, d + (n + 1) * LRU_COL_BLOCK)
        return (jnp.dot(hp, inw_ref[:, cols], preferred_element_type=F32),
                jnp.dot(hp, inw_ref[:, zcols], preferred_element_type=F32))

    def out_proj(tile, yn, n):
        b, s = tile
        rows = pl.ds(s * tm, tm)
        cols = slice(n * LRU_COL_BLOCK, (n + 1) * LRU_COL_BLOCK)
        out = jnp.dot(yn, outw_ref[:, cols], preferred_element_type=F32)
        o_ref[b, rows, cols] = x_ref[b, rows, cols] + cond_ref[b, 2:3, cols] * out

    def unpermute(y_blocks):
        y = jnp.concatenate(y_blocks, axis=1)
        return jnp.dot(permt_ref[...], y, preferred_element_type=F32).astype(BF16)

    hp = permuted_input(*tiles[0])
    proj = [in_proj(hp, n) for n in range(ncol)]
    prev, yn_prev = None, None
    for idx, (b, s) in enumerate(tiles):
        nxt = tiles[idx + 1] if idx + 1 < len(tiles) else None
        if nxt is not None:
            hp = permuted_input(*nxt)
        proj_next, y_blocks = [], []
        def recurrence(n, half):
            lo = half * LRU_BLOCK_W
            cols = slice(n * LRU_COL_BLOCK + lo, n * LRU_COL_BLOCK + lo + LRU_BLOCK_W)
            blk = n * (LRU_COL_BLOCK // LRU_BLOCK_W) + half
            y_blocks.append(_lru_recurrence_block(
                proj[n][0][:, lo:lo + LRU_BLOCK_W], proj[n][1][:, lo:lo + LRU_BLOCK_W],
                cw_ref.at[:, cols], cb_ref.at[:, cols], gw_ref.at[pl.ds(blk, 1)],
                gb_ref.at[:, cols], lam_ref.at[:, cols], tail_ref.at[b, :, cols],
                carry_ref.at[b, :, cols], tm=tm))

        for n in range(ncol):
            if nxt is not None:
                proj_next.append(in_proj(hp, n))
            recurrence(n, 0)
            if prev is not None:
                out_proj(prev, yn_prev, n)
            recurrence(n, 1)
        proj = proj_next
        prev, yn_prev = (b, s), unpermute(y_blocks)
    for n in range(ncol):
        out_proj(prev, yn_prev, n)


def _lru_recurrence_block(xb, zg, cw_ref, cb_ref, gw_ref, gb_ref, lam_ref, tail_ref, carry_ref, *, tm):
    g_rows = tm // SUBLANES
    w = xb.shape[1]
    n_tail = (CONV_WIDTH - 1) * SUBLANES
    cur_tail = xb[tm - n_tail:, :]
    prev_tail = tail_ref[...]
    sub = lax.broadcasted_iota(jnp.int32, (SUBLANES, w), 0)
    heads = []
    for m in range(CONV_WIDTH - 1):
        cur = pltpu.roll(cur_tail[m * SUBLANES:(m + 1) * SUBLANES, :], 1, 0)
        prv = pltpu.roll(prev_tail[m * SUBLANES:(m + 1) * SUBLANES, :], 1, 0)
        heads.append(jnp.where(sub == 0, prv, cur))
    tail_ref[...] = cur_tail
    ext = jnp.concatenate(heads + [xb], axis=0)
    xf = cb_ref[...]
    for k in range(CONV_WIDTH):
        off = n_tail - k * SUBLANES
        xf = xf + cw_ref[CONV_WIDTH - 1 - k:CONV_WIDTH - k, :] * ext[off:off + tm, :]

    xfb = xf.astype(BF16)
    pre_r, pre_i = [], []
    for n in range(w // LRU_BLOCK_W):
        pre = jnp.dot(xfb[:, n * LRU_BLOCK_W:(n + 1) * LRU_BLOCK_W], gw_ref[n],
                      preferred_element_type=F32)
        pre_r.append(pre[:, :LRU_BLOCK_W])
        pre_i.append(pre[:, LRU_BLOCK_W:])
    tanh_r = jnp.tanh(jnp.concatenate(pre_r, axis=1) + gb_ref[0:1, :])
    tanh_i = jnp.tanh(jnp.concatenate(pre_i, axis=1) + gb_ref[1:2, :])
    half_c = (-0.5 * RG_C) * _softplus(-lam_ref[...])
    log_a = half_c * tanh_r + half_c
    a_t = jnp.exp(log_a)
    th = jnp.tanh(log_a)
    msq = -0.5 * th / (1.0 - th)
    half_mult = jnp.where(msq > 0.0, msq * lax.rsqrt(msq), 0.0)
    b_t = (half_mult * xf) * (tanh_i + 1.0)

    h_loc = jnp.zeros((SUBLANES, w), F32)
    p_loc = jnp.ones((SUBLANES, w), F32)
    h_rows, p_rows = [], []
    for i in range(g_rows):
        a_i = a_t[i * SUBLANES:(i + 1) * SUBLANES, :]
        h_loc = a_i * h_loc + b_t[i * SUBLANES:(i + 1) * SUBLANES, :]
        p_loc = a_i * p_loc
        h_rows.append(h_loc)
        p_rows.append(p_loc)
    c_in = jnp.broadcast_to(carry_ref[SUBLANES - 1:SUBLANES, :], (SUBLANES, w))
    c_sub = c_in
    for _ in range(SUBLANES - 1):
        c_sub = jnp.where(sub == 0, c_in, pltpu.roll(h_loc + p_loc * c_sub, 1, 0))
    carry_ref[...] = h_loc + p_loc * c_sub
    hseq = jnp.concatenate([hr + pr * c_sub for hr, pr in zip(h_rows, p_rows)], axis=0)
    return ((hseq * zg) * (jnp.tanh(zg) + 1.0)).astype(BF16)


def _time_permutation(tm):
    g_rows = tm // SUBLANES
    r = np.arange(tm)
    src = (r % SUBLANES) * g_rows + r // SUBLANES
    p = np.zeros((tm, tm), np.float32)
    p[r, src] = 1.0
    return p


def _lru_layer(x, cond_i, norm_g, in_w_all, conv_w, conv_b, gw_all, gate_b, lam, out_w_all, j, *, nb,
               ns, tm):
    bsz, s, d = x.shape
    nblk = d // LRU_BLOCK_W
    perm = _time_permutation(tm)
    const = lambda *shape: pl.BlockSpec(shape, lambda b, t: (0,) * len(shape))
    layer = lambda *shape: pl.BlockSpec((None,) + shape, lambda b, t: (j,) + (0,) * len(shape))
    kern = functools.partial(_lru_kernel, nb=nb, ns=ns, tm=tm, d=d)
    return pl.pallas_call(
        kern,
        grid=(bsz // nb, s // (ns * tm)),
        in_specs=[pl.BlockSpec((nb, ns * tm, d), lambda b, t: (b, t, 0)),
                  pl.BlockSpec((nb, 3, d), lambda b, t: (b, 0, 0)),
                  const(1, d), const(tm, tm), const(tm, tm), layer(d, 2 * d), const(CONV_WIDTH, d),
                  const(1, d), layer(nblk, LRU_BLOCK_W, 2 * LRU_BLOCK_W), const(2, d), const(1, d),
                  layer(d, d)],
        out_specs=pl.BlockSpec((nb, ns * tm, d), lambda b, t: (b, t, 0)),
        out_shape=jax.ShapeDtypeStruct((bsz, s, d), F32),
        scratch_shapes=[pltpu.VMEM((nb, (CONV_WIDTH - 1) * SUBLANES, d), F32),
                        pltpu.VMEM((nb, SUBLANES, d), F32)],
        compiler_params=pltpu.CompilerParams(dimension_semantics=("arbitrary", "arbitrary"),
                                             vmem_limit_bytes=VMEM_LIMIT_BYTES),
        name="rglru_layer",
    )(x, cond_i, norm_g.reshape(1, d), jnp.asarray(perm, BF16), jnp.asarray(perm.T, BF16),
      in_w_all, conv_w, conv_b.reshape(1, d), gw_all, gate_b, lam.reshape(1, d), out_w_all)


def _gdn_kernel(x_ref, cond_ref, ng_ref, wqkvz_ref, wab_ref, wabt_ref, cw_ref, alog_ref,
                dtb_ref, alogr_ref, dtbr_ref, tri_ref, trit_ref, og_ref, outw_ref, fg_ref, o_ref,
                hist_ref, state_ref, qp_ref, kp_ref, vp_ref,
                *, nb, ns, tm, d, nheads, final_norm):
    c_len = GDN_CHUNK
    nchunks = tm // c_len
    npairs = nheads // 2
    n_mat = nchunks * npairs
    hk = nheads * HEAD_DIM
    t = pl.program_id(1)

    @pl.when(t == 0)
    def _():
        hist_ref[:, :SUBLANES, :] = jnp.zeros((nb, SUBLANES, 3 * hk), F32)
        state_ref[...] = jnp.zeros_like(state_ref)

    rr = lax.broadcasted_iota(jnp.int32, (2 * c_len, 2 * c_len), 0)
    cc = lax.broadcasted_iota(jnp.int32, (2 * c_len, 2 * c_len), 1)
    same = (rr >= c_len) == (cc >= c_len)
    causal = same & (rr >= cc)
    strict = same & (rr > cc)
    eye = (rr == cc).astype(F32)
    merge_masks = []
    blk = 1
    while blk < c_len:
        merge_masks.append(((rr ^ cc) < 2 * blk) & ((rr & blk) != 0) & ((cc & blk) == 0))
        blk *= 2

    def stage_a(tile, slot, ctx):
        b, s = tile
        rows = pl.ds(s * tm, tm)
        hb = _modulated_rmsnorm(x_ref[b, rows, :], ng_ref[...], cond_ref[b, 0:1, :],
                                cond_ref[b, 1:2, :]).astype(BF16)
        yield
        z_blocks = []
        for j in range(hk // GDN_COL_BLOCK):
            zcols = slice(3 * hk + j * GDN_COL_BLOCK, 3 * hk + (j + 1) * GDN_COL_BLOCK)
            z_blocks.append(jnp.dot(hb, wqkvz_ref[:, zcols], preferred_element_type=F32))
            yield
        ctx["z"] = jnp.concatenate(z_blocks, axis=1)
        heads_per_piece = GDN_COL_BLOCK // HEAD_DIM
        for j in range(3 * hk // GDN_COL_BLOCK):
            cols = slice(j * GDN_COL_BLOCK, (j + 1) * GDN_COL_BLOCK)
            pre = jnp.dot(hb, wqkvz_ref[:, cols], preferred_element_type=F32)
            hist_ref[b, SUBLANES:, cols] = pre
            conv = cw_ref[CONV_WIDTH - 1:CONV_WIDTH, cols] * pre
            for k in range(1, CONV_WIDTH):
                conv = conv + (cw_ref[CONV_WIDTH - 1 - k:CONV_WIDTH - k, cols]
                               * hist_ref[b, SUBLANES - k:SUBLANES - k + tm, cols])
            hist_ref[b, :SUBLANES, cols] = pre[tm - SUBLANES:, :]
            act = conv * (jnp.tanh(conv) + 1.0)
            for i in range(heads_per_piece):
                which, h = divmod(j * heads_per_piece + i, nheads)
                a_h = act[:, i * HEAD_DIM:(i + 1) * HEAD_DIM]
                if which < 2:
                    inv = lax.rsqrt(jnp.sum(a_h * a_h, axis=-1, keepdims=True) + EPS)
                    a_h = a_h * (inv * (HEAD_DIM ** -0.5) if which == 0 else inv)
                dst_ref = (qp_ref, kp_ref, vp_ref)[which]
                hp, par = divmod(h, 2)
                for c in range(nchunks):
                    dst_ref[slot, c, hp, par * c_len:(par + 1) * c_len, :] = (
                        a_h[c * c_len:(c + 1) * c_len, :])
            yield
        ab = jnp.dot(hb, wab_ref[...], preferred_element_type=F32)
        abt = lax.dot_general(wabt_ref[...], hb, (((1,), (1,)), ((), ())),
                              preferred_element_type=F32)
        g_col = -jnp.exp(alog_ref[...]) * _softplus(ab[:, :LANES] + dtb_ref[...])
        g_row = -jnp.exp(alogr_ref[...]) * _softplus(abt + dtbr_ref[...])
        ctx["beta"] = _sigmoid(ab[:, LANES:])
        gcol_parts = _split3(g_col)
        grow_parts = _split3(g_row)
        gc_blocks, gcr_blocks = [], []
        for i in range(tm // LANES):
            sl = slice(i * LANES, (i + 1) * LANES)
            gc_blocks.append(sum(jnp.dot(tri_ref[...], p[sl, :], preferred_element_type=F32)
                                 for p in gcol_parts))
            gcr_blocks.append(sum(jnp.dot(p[:, sl], trit_ref[...], preferred_element_type=F32)
                                  for p in grow_parts))
        ctx["gc"] = jnp.concatenate(gc_blocks, axis=0)
        ctx["gcr"] = jnp.concatenate(gcr_blocks, axis=1)
        yield

    def pair_cols(arr):
        mats = []
        for c in range(nchunks):
            blk_rows = arr[c * c_len:(c + 1) * c_len, :]
            for hp in range(npairs):
                mats.append(jnp.concatenate(
                    [jnp.broadcast_to(blk_rows[:, 2 * hp + par:2 * hp + par + 1], (c_len, HEAD_DIM))
                     for par in range(2)], axis=0))
        return jnp.stack(mats)

    def pair_rows(arr):
        mats = []
        for c in range(nchunks):
            for hp in range(npairs):
                mats.append(jnp.concatenate(
                    [arr[2 * hp + par:2 * hp + par + 1, c * c_len:(c + 1) * c_len]
                     for par in range(2)], axis=1))
        return jnp.stack(mats)

    def merge_level(t_mat, a_mat, m, blk):
        n = 2 * c_len
        if blk < SUBLANES:
            return t_mat - _bmm(t_mat, _bmm(jnp.where(m[None], a_mat, 0.0), t_mat))
        ups = [slice(r, r + blk) for r in range(0, n, 2 * blk)]
        lows = [slice(r, r + blk) for r in range(blk, n, 2 * blk)]
        o_lo = jnp.concatenate([jnp.where(m[sl][None], a_mat[:, sl], 0.0) for sl in lows], axis=1)
        x_lo = _bmm(o_lo, t_mat)
        zeros = jnp.zeros((n_mat, blk, n), F32)
        x_full = jnp.concatenate(
            [piece for i in range(len(lows)) for piece in (zeros, x_lo[:, i * blk:(i + 1) * blk])],
            axis=1)
        upd = _bmm(jnp.concatenate([t_mat[:, sl] for sl in lows], axis=1), x_full)
        return jnp.concatenate(
            [piece for i, (up, lo) in enumerate(zip(ups, lows))
             for piece in (t_mat[:, up], t_mat[:, lo] - upd[:, i * blk:(i + 1) * blk])], axis=1)

    def stage_b(tile, slot, ctx):
        b, s = tile
        rows = pl.ds(s * tm, tm)
        q2 = qp_ref[slot].reshape(n_mat, 2 * c_len, HEAD_DIM)
        k2 = kp_ref[slot].reshape(n_mat, 2 * c_len, HEAD_DIM)
        v2 = vp_ref[slot].reshape(n_mat, 2 * c_len, HEAD_DIM)
        beta2 = pair_cols(ctx["beta"])
        gcol2 = pair_cols(ctx["gc"])
        grow2 = pair_rows(ctx["gcr"])
        yield
        eg2 = jnp.exp(gcol2)
        glast2 = jnp.concatenate(
            [jnp.broadcast_to(gcol2[:, (par + 1) * c_len - 1:(par + 1) * c_len, :],
                              (n_mat, c_len, HEAD_DIM)) for par in range(2)], axis=1)
        kd2 = jnp.exp(glast2 - gcol2)
        decay = jnp.exp(jnp.where(causal[None], gcol2 - grow2, -1e30))
        yield
        kb2 = k2 * beta2
        gram = _bmm_nt(jnp.concatenate([kb2, q2], axis=1), k2)
        a_mat = jnp.where(strict[None], gram[:, :2 * c_len] * decay, 0.0)
        attn = gram[:, 2 * c_len:] * decay
        yield
        t_mat = eye[None] - jnp.where(merge_masks[0][None], a_mat, 0.0)
        for lvl, m in enumerate(merge_masks[1:], start=1):
            t_mat = merge_level(t_mat, a_mat, m, 2 ** lvl)
            yield
        wu = _bmm(t_mat, jnp.concatenate([kb2 * eg2, v2 * beta2], axis=2))
        w2 = wu[:, :, :HEAD_DIM]
        u2 = wu[:, :, HEAD_DIM:]
        qd2 = q2 * eg2
        kdec2 = k2 * kd2
        yield
        st = slice(b * npairs, (b + 1) * npairs)
        o_chunks = []
        for c in range(nchunks):
            sl = slice(c * npairs, (c + 1) * npairs)
            vn_parts, qs_parts = [], []
            for par in range(2):
                rs = slice(par * c_len, (par + 1) * c_len)
                wq = _bmm(jnp.concatenate([w2[sl, rs], qd2[sl, rs]], axis=1), state_ref[st, par])
                vn_parts.append(u2[sl, rs] - wq[:, :c_len])
                qs_parts.append(wq[:, c_len:])
            v_new = jnp.concatenate(vn_parts, axis=1)
            o_chunks.append(jnp.concatenate(qs_parts, axis=1) + _bmm(attn[sl], v_new))
            for par in range(2):
                rs = slice(par * c_len, (par + 1) * c_len)
                e_last = jnp.broadcast_to(eg2[sl, (par + 1) * c_len - 1:(par + 1) * c_len, :],
                                          (npairs, HEAD_DIM, HEAD_DIM))
                state_ref[st, par] = (state_ref[st, par] * e_last
                                      + _bmm_tn(kdec2[sl, rs], v_new[:, rs]))
            yield
        parts = []
        for h in range(nheads):
            hp, par = divmod(h, 2)
            oh = jnp.concatenate([oc[hp, par * c_len:(par + 1) * c_len, :] for oc in o_chunks],
                                 axis=0)
            parts.append(oh * lax.rsqrt(jnp.mean(oh * oh, axis=-1, keepdims=True) + EPS))
        z = ctx["z"]
        og = (jnp.concatenate(parts, axis=1) * og_ref[...] * (z * (jnp.tanh(z) + 1.0))).astype(BF16)
        yield
        xn_blocks = []
        for j in range(d // GDN_COL_BLOCK):
            cols = slice(j * GDN_COL_BLOCK, (j + 1) * GDN_COL_BLOCK)
            out = jnp.dot(og, outw_ref[:, cols], preferred_element_type=F32)
            xn = x_ref[b, rows, cols] + cond_ref[b, 2:3, cols] * out
            if final_norm:
                xn_blocks.append(xn)
            else:
                o_ref[b, rows, cols] = xn
            yield
        if final_norm:
            xn = jnp.concatenate(xn_blocks, axis=1)
            ms = jnp.mean(xn * xn, axis=-1, keepdims=True)
            o_ref[b, rows, :] = xn * lax.rsqrt(ms + EPS) * fg_ref[...]

    tiles = [(b, s) for s in range(ns) for b in range(nb)]
    ctxs = [dict() for _ in tiles]
    order = "abab" + "a" + "ab" + "a" + "ab" * 10 + "aaa"
    _interleave("", a=stage_a(tiles[0], 0, ctxs[0]))
    for idx, tile in enumerate(tiles):
        gens = dict(b=stage_b(tile, idx % 2, ctxs[idx]))
        if idx + 1 < len(tiles):
            gens["a"] = stage_a(tiles[idx + 1], (idx + 1) % 2, ctxs[idx + 1])
        _interleave(order, **gens)


def _pair_tri():
    r = np.arange(LANES)
    m = (r[:, None] >= r[None, :]) & (r[:, None] // GDN_CHUNK == r[None, :] // GDN_CHUNK)
    return m.astype(np.float32)


def _gdn_layer(x, cond_i, norm_g, wqkvz_all, w_ab, conv_w, a_log, dt_bias, onorm_g, out_w_all,
               final_g, j, *, nb, ns, tm, final_norm):
    bsz, s, d = x.shape
    nheads = a_log.shape[0]
    hk = nheads * HEAD_DIM
    nchunks = tm // GDN_CHUNK
    npairs = nheads // 2
    o1 = 3 * hk
    wa = w_ab[:, :nheads]
    wb = w_ab[:, nheads:]
    pad_l = lambda w: jnp.pad(w, ((0, 0), (0, LANES - nheads)))
    wab = jnp.concatenate([pad_l(wa), pad_l(wb)], axis=1).astype(BF16)
    wabt = w_ab.T.astype(BF16)
    lane_vec = lambda v: jnp.pad(v, (0, LANES - nheads)).reshape(1, LANES)
    row_vec = lambda v: jnp.broadcast_to(jnp.pad(v, (0, nheads))[:, None], (2 * nheads, tm))
    tri = _pair_tri()
    const = lambda *shape: pl.BlockSpec(shape, lambda b, t: (0,) * len(shape))
    layer = lambda *shape: pl.BlockSpec((None,) + shape, lambda b, t: (j,) + (0,) * len(shape))
    kern = functools.partial(_gdn_kernel, nb=nb, ns=ns, tm=tm, d=d, nheads=nheads,
                             final_norm=final_norm)
    pair_stack = pltpu.VMEM((2, nchunks, npairs, 2 * GDN_CHUNK, HEAD_DIM), F32)
    return pl.pallas_call(
        kern,
        grid=(bsz // nb, s // (ns * tm)),
        in_specs=[pl.BlockSpec((nb, ns * tm, d), lambda b, t: (b, t, 0)),
                  pl.BlockSpec((nb, 3, d), lambda b, t: (b, 0, 0)),
                  const(1, d), layer(d, wqkvz_all.shape[2]), const(d, 2 * LANES),
                  const(2 * nheads, d), const(CONV_WIDTH, o1), const(1, LANES), const(1, LANES),
                  const(2 * nheads, tm), const(2 * nheads, tm), const(LANES, LANES),
                  const(LANES, LANES), const(1, hk), layer(hk, d), const(1, d)],
        out_specs=pl.BlockSpec((nb, ns * tm, d), lambda b, t: (b, t, 0)),
        out_shape=jax.ShapeDtypeStruct((bsz, s, d), F32),
        scratch_shapes=[pltpu.VMEM((nb, SUBLANES + tm, o1), F32),
                        pltpu.VMEM((nb * npairs, 2, HEAD_DIM, HEAD_DIM), F32),
                        pair_stack, pair_stack, pair_stack],
        compiler_params=pltpu.CompilerParams(dimension_semantics=("arbitrary", "arbitrary"),
                                             vmem_limit_bytes=VMEM_LIMIT_BYTES),
        name="gdn_layer",
    )(x, cond_i, norm_g.reshape(1, d), wqkvz_all, wab, wabt, conv_w, lane_vec(a_log),
      lane_vec(dt_bias), row_vec(a_log), row_vec(dt_bias), jnp.asarray(tri, BF16),
      jnp.asarray(tri.T, BF16), jnp.tile(onorm_g, nheads).reshape(1, hk), out_w_all,
      final_g.reshape(1, d))


def kernel(x, c, ada_w, ada_b, norm_g, final_g, lru_in_w, lru_conv_w, lru_conv_b, lru_gate_w,
           lru_gate_b, lru_lambda, lru_out_w, gdn_in_w, gdn_conv_w, gdn_a_log, gdn_dt_bias,
           gdn_onorm_g, gdn_out_w):
    depth = ada_w.shape[0]
    assert depth % 2 == 0, "layers alternate RG-LRU / DeltaNet and the last one applies the final norm"
    cond = _cond(c, ada_w, ada_b)
    d = x.shape[-1]
    n_hv = gdn_a_log.shape[1] * HEAD_DIM
    n_qkv = 3 * n_hv
    halve_tail = lambda n_keep, n_all: jnp.where(jnp.arange(n_all) < n_keep, 1.0, 0.5).astype(F32)
    lru_in_bf = (lru_in_w * halve_tail(d, lru_in_w.shape[-1])).astype(BF16)
    lru_out_bf = lru_out_w.astype(BF16)
    lru_gw_bf = (0.5 * jnp.concatenate([lru_gate_w[:, 0], lru_gate_w[:, 1]], axis=-1)).astype(BF16)
    z_scale = jnp.where((jnp.arange(gdn_in_w.shape[-1]) >= n_qkv)
                        & (jnp.arange(gdn_in_w.shape[-1]) < n_qkv + n_hv), 0.5, 1.0).astype(F32)
    gdn_in_bf = (gdn_in_w * z_scale).astype(BF16)
    gdn_out_bf = gdn_out_w.astype(BF16)
    for i in range(depth):
        j = i // 2
        if i % 2 == 0:
            x = _lru_layer(x, cond[i], norm_g[i], lru_in_bf, lru_conv_w[j], lru_conv_b[j], lru_gw_bf,
                           0.5 * lru_gate_b[j], lru_lambda[j], lru_out_bf, j, nb=1, ns=4, tm=256)
        else:
            x = _gdn_layer(x, cond[i], norm_g[i], gdn_in_bf, gdn_in_w[j, :, n_qkv + n_hv:],
                           0.5 * gdn_conv_w[j], gdn_a_log[j], gdn_dt_bias[j], gdn_onorm_g[j],
                           gdn_out_bf, final_g, j, nb=1, ns=2, tm=256, final_norm=(i == depth - 1))
    return x
```

```python
import functools

import numpy as np
import jax
import jax.numpy as jnp
from jax import lax
from jax.experimental import pallas as pl
from jax.experimental.pallas import tpu as pltpu

F32 = jnp.float32
BF16 = jnp.bfloat16

EPS = 1e-6
RG_C = 8.0
CONV_WIDTH = 4
SUBLANES = 8
LANES = 128
GDN_CHUNK = 64
HEAD_DIM = 128
LRU_BLOCK_W = 128
LRU_COL_BLOCK = 256
GDN_COL_BLOCK = 256
VMEM_LIMIT_BYTES = 56 * 1024 * 1024


def _sigmoid(v):
    return 0.5 * jnp.tanh(0.5 * v) + 0.5


def _softplus(v):
    return jnp.maximum(v, 0.0) + jnp.log1p(jnp.exp(-jnp.abs(v)))


def _bdot(a, b):
    return jnp.dot(a.astype(BF16), b.astype(BF16), preferred_element_type=F32)


def _bmm(a, b):
    return lax.dot_general(a.astype(BF16), b.astype(BF16), (((2,), (1,)), ((0,), (0,))),
                           preferred_element_type=F32)


def _bmm_nt(a, b):
    return lax.dot_general(a.astype(BF16), b.astype(BF16), (((2,), (2,)), ((0,), (0,))),
                           preferred_element_type=F32)


def _bmm_tn(a, b):
    return lax.dot_general(a.astype(BF16), b.astype(BF16), (((1,), (1,)), ((0,), (0,))),
                           preferred_element_type=F32)


def _split3(v):
    v1 = v.astype(BF16)
    r1 = v - v1.astype(F32)
    v2 = r1.astype(BF16)
    v3 = (r1 - v2.astype(F32)).astype(BF16)
    return v1, v2, v3


def _modulated_rmsnorm(x, g, shift, scale):
    ms = jnp.mean(x * x, axis=-1, keepdims=True)
    return (x * lax.rsqrt(ms + EPS)) * (g * (1.0 + scale)) + shift


def _interleave(order, **gens):
    def step(name):
        try:
            next(gens[name])
        except StopIteration:
            gens.pop(name)

    for name in order:
        if name in gens:
            step(name)
    while gens:
        for name in list(gens):
            step(name)


def _cond_kernel(c_ref, w_ref, b_ref, o_ref):
    c = c_ref[...]
    ca = c * _sigmoid(c)
    o_ref[0] = _bdot(ca, w_ref[0]) + b_ref[0]


def _cond(c, ada_w, ada_b):
    depth, d, d3 = ada_w.shape
    bsz = c.shape[0]
    bp = -(-bsz // SUBLANES) * SUBLANES
    c8 = jnp.pad(c, ((0, bp - bsz), (0, 0)))
    tn = d
    out = pl.pallas_call(
        _cond_kernel,
        grid=(depth, d3 // tn),
        in_specs=[pl.BlockSpec((bp, d), lambda i, j: (0, 0)),
                  pl.BlockSpec((1, d, tn), lambda i, j: (i, 0, j)),
                  pl.BlockSpec((1, 1, tn), lambda i, j: (i, 0, j))],
        out_specs=pl.BlockSpec((1, bp, tn), lambda i, j: (i, 0, j)),
        out_shape=jax.ShapeDtypeStruct((depth, bp, d3), F32),
        name="adaln_cond",
    )(c8, ada_w, ada_b.reshape(depth, 1, d3))
    return out[:, :bsz].reshape(depth, bsz, 3, d)


def _lru_kernel(x_ref, cond_ref, ng_ref, perm_ref, permt_ref, inw_ref, cw_ref, cb_ref, gw_ref,
                gb_ref, lam_ref, outw_ref, o_ref, tail_ref, carry_ref, *, nb, ns, tm, d):
    t = pl.program_id(1)

    @pl.when(t == 0)
    def _():
        tail_ref[...] = jnp.zeros_like(tail_ref)
        carry_ref[...] = jnp.zeros_like(carry_ref)

    ncol = d // LRU_COL_BLOCK
    tiles = [(b, s) for s in range(ns) for b in range(nb)]

    def permuted_input(b, s):
        x = x_ref[b, pl.ds(s * tm, tm), :]
        h = _modulated_rmsnorm(x, ng_ref[...], cond_ref[b, 0:1, :], cond_ref[b, 1:2, :])
        return jnp.dot(perm_ref[...], h.astype(BF16), preferred_element_type=F32).astype(BF16)

    def in_proj(hp, n):
        cols = slice(n * LRU_COL_BLOCK, (n + 1) * LRU_COL_BLOCK)
        zcols = slice(d + n * LRU_COL_BLOCK, d + (n + 1) * LRU_COL_BLOCK)
        return (jnp.dot(hp, inw_ref[:, cols], preferred_element_type=F32),
                jnp.dot(hp, inw_ref[:, zcols], preferred_element_type=F32))

    def out_proj(tile, yn, n):
        b, s = tile
        rows = pl.ds(s * tm, tm)
        cols = slice(n * LRU_COL_BLOCK, (n + 1) * LRU_COL_BLOCK)
        out = jnp.dot(yn, outw_ref[:, cols], preferred_element_type=F32)
        o_ref[b, rows, cols] = x_ref[b, rows, cols] + cond_ref[b, 2:3, cols] * out

    def unpermute(y_blocks):
        y = jnp.concatenate(y_blocks, axis=1)
        return jnp.dot(permt_ref[...], y, preferred_element_type=F32).astype(BF16)

    hp = permuted_input(*tiles[0])
    proj = [in_proj(hp, n) for n in range(ncol)]
    prev, yn_prev = None, None
    for idx, (b, s) in enumerate(tiles):
        nxt = tiles[idx + 1] if idx + 1 < len(tiles) else None
        if nxt is not None:
            hp = permuted_input(*nxt)
        proj_next, y_blocks = [], []
        def recurrence(n, half):
            lo = half * LRU_BLOCK_W
            cols = slice(n * LRU_COL_BLOCK + lo, n * LRU_COL_BLOCK + lo + LRU_BLOCK_W)
            blk = n * (LRU_COL_BLOCK // LRU_BLOCK_W) + half
            y_blocks.append(_lru_recurrence_block(
                proj[n][0][:, lo:lo + LRU_BLOCK_W], proj[n][1][:, lo:lo + LRU_BLOCK_W],
                cw_ref.at[:, cols], cb_ref.at[:, cols], gw_ref.at[pl.ds(blk, 1)],
                gb_ref.at[:, cols], lam_ref.at[:, cols], tail_ref.at[b, :, cols],
                carry_ref.at[b, :, cols], tm=tm))

        for n in range(ncol):
            if nxt is not None:
                proj_next.append(in_proj(hp, n))
            recurrence(n, 0)
            if prev is not None:
                out_proj(prev, yn_prev, n)
            recurrence(n, 1)
        proj = proj_next
        prev, yn_prev = (b, s), unpermute(y_blocks)
    for n in range(ncol):
        out_proj(prev, yn_prev, n)


def _lru_recurrence_block(xb, zg, cw_ref, cb_ref, gw_ref, gb_ref, lam_ref, tail_ref, carry_ref, *, tm):
    g_rows = tm // SUBLANES
    w = xb.shape[1]
    n_tail = (CONV_WIDTH - 1) * SUBLANES
    cur_tail = xb[tm - n_tail:, :]
    prev_tail = tail_ref[...]
    sub = lax.broadcasted_iota(jnp.int32, (SUBLANES, w), 0)
    heads = []
    for m in range(CONV_WIDTH - 1):
        cur = pltpu.roll(cur_tail[m * SUBLANES:(m + 1) * SUBLANES, :], 1, 0)
        prv = pltpu.roll(prev_tail[m * SUBLANES:(m + 1) * SUBLANES, :], 1, 0)
        heads.append(jnp.where(sub == 0, prv, cur))
    tail_ref[...] = cur_tail
    ext = jnp.concatenate(heads + [xb], axis=0)
    xf = cb_ref[...]
    for k in range(CONV_WIDTH):
        off = n_tail - k * SUBLANES
        xf = xf + cw_ref[CONV_WIDTH - 1 - k:CONV_WIDTH - k, :] * ext[off:off + tm, :]

    xfb = xf.astype(BF16)
    pre_r, pre_i = [], []
    for n in range(w // LRU_BLOCK_W):
        pre = jnp.dot(xfb[:, n * LRU_BLOCK_W:(n + 1) * LRU_BLOCK_W], gw_ref[n],
                      preferred_element_type=F32)
        pre_r.append(pre[:, :LRU_BLOCK_W])
        pre_i.append(pre[:, LRU_BLOCK_W:])
    tanh_r = jnp.tanh(jnp.concatenate(pre_r, axis=1) + gb_ref[0:1, :])
    tanh_i = jnp.tanh(jnp.concatenate(pre_i, axis=1) + gb_ref[1:2, :])
    half_c = (-0.5 * RG_C) * _softplus(-lam_ref[...])
    log_a = half_c * tanh_r + half_c
    a_t = jnp.exp(log_a)
    th = jnp.tanh(log_a)
    msq = -0.5 * th / (1.0 - th)
    half_mult = jnp.where(msq > 0.0, msq * lax.rsqrt(msq), 0.0)
    b_t = (half_mult * xf) * (tanh_i + 1.0)

    h_loc = jnp.zeros((SUBLANES, w), F32)
    p_loc = jnp.ones((SUBLANES, w), F32)
    h_rows, p_rows = [], []
    for i in range(g_rows):
        a_i = a_t[i * SUBLANES:(i + 1) * SUBLANES, :]
        h_loc = a_i * h_loc + b_t[i * SUBLANES:(i + 1) * SUBLANES, :]
        p_loc = a_i * p_loc
        h_rows.append(h_loc)
        p_rows.append(p_loc)
    c_in = jnp.broadcast_to(carry_ref[SUBLANES - 1:SUBLANES, :], (SUBLANES, w))
    c_sub = c_in
    for _ in range(SUBLANES - 1):
        c_sub = jnp.where(sub == 0, c_in, pltpu.roll(h_loc + p_loc * c_sub, 1, 0))
    carry_ref[...] = h_loc + p_loc * c_sub
    hseq = jnp.concatenate([hr + pr * c_sub for hr, pr in zip(h_rows, p_rows)], axis=0)
    return ((hseq * zg) * (jnp.tanh(zg) + 1.0)).astype(BF16)


def _time_permutation(tm):
    g_rows = tm // SUBLANES
    r = np.arange(tm)
    src = (r % SUBLANES) * g_rows + r // SUBLANES
    p = np.zeros((tm, tm), np.float32)
    p[r, src] = 1.0
    return p


def _lru_layer(x, cond_i, norm_g, in_w_all, conv_w, conv_b, gw_all, gate_b, lam, out_w_all, j, *, nb,
               ns, tm):
    bsz, s, d = x.shape
    nblk = d // LRU_BLOCK_W
    perm = _time_permutation(tm)
    const = lambda *shape: pl.BlockSpec(shape, lambda b, t: (0,) * len(shape))
    layer = lambda *shape: pl.BlockSpec((None,) + shape, lambda b, t: (j,) + (0,) * len(shape))
    kern = functools.partial(_lru_kernel, nb=nb, ns=ns, tm=tm, d=d)
    return pl.pallas_call(
        kern,
        grid=(bsz // nb, s // (ns * tm)),
        in_specs=[pl.BlockSpec((nb, ns * tm, d), lambda b, t: (b, t, 0)),
                  pl.BlockSpec((nb, 3, d), lambda b, t: (b, 0, 0)),
                  const(1, d), const(tm, tm), const(tm, tm), layer(d, 2 * d), const(CONV_WIDTH, d),
                  const(1, d), layer(nblk, LRU_BLOCK_W, 2 * LRU_BLOCK_W), const(2, d), const(1, d),
                  layer(d, d)],
        out_specs=pl.BlockSpec((nb, ns * tm, d), lambda b, t: (b, t, 0)),
        out_shape=jax.ShapeDtypeStruct((bsz, s, d), F32),
        scratch_shapes=[pltpu.VMEM((nb, (CONV_WIDTH - 1) * SUBLANES, d), F32),
                        pltpu.VMEM((nb, SUBLANES, d), F32)],
        compiler_params=pltpu.CompilerParams(dimension_semantics=("arbitrary", "arbitrary"),
                                             vmem_limit_bytes=VMEM_LIMIT_BYTES),
        name="rglru_layer",
    )(x, cond_i, norm_g.reshape(1, d), jnp.asarray(perm, BF16), jnp.asarray(perm.T, BF16),
      in_w_all, conv_w, conv_b.reshape(1, d), gw_all, gate_b, lam.reshape(1, d), out_w_all)


def _gdn_kernel(x_ref, cond_ref, ng_ref, wqkvz_ref, wab_ref, wabt_ref, cw_ref, alog_ref,
                dtb_ref, alogr_ref, dtbr_ref, tri_ref, trit_ref, og_ref, outw_ref, fg_ref, o_ref,
                hist_ref, state_ref, qp_ref, kp_ref, vp_ref,
                *, nb, ns, tm, d, nheads, final_norm):
    c_len = GDN_CHUNK
    nchunks = tm // c_len
    npairs = nheads // 2
    n_mat = nchunks * npairs
    hk = nheads * HEAD_DIM
    t = pl.program_id(1)

    @pl.when(t == 0)
    def _():
        hist_ref[:, :SUBLANES, :] = jnp.zeros((nb, SUBLANES, 3 * hk), F32)
        state_ref[...] = jnp.zeros_like(state_ref)

    rr = lax.broadcasted_iota(jnp.int32, (2 * c_len, 2 * c_len), 0)
    cc = lax.broadcasted_iota(jnp.int32, (2 * c_len, 2 * c_len), 1)
    same = (rr >= c_len) == (cc >= c_len)
    causal = same & (rr >= cc)
    strict = same & (rr > cc)
    eye = (rr == cc).astype(F32)
    merge_masks = []
    blk = 1
    while blk < c_len:
        merge_masks.append(((rr ^ cc) < 2 * blk) & ((rr & blk) != 0) & ((cc & blk) == 0))
        blk *= 2

    def stage_a(tile, slot, ctx):
        b, s = tile
        rows = pl.ds(s * tm, tm)
        hb = _modulated_rmsnorm(x_ref[b, rows, :], ng_ref[...], cond_ref[b, 0:1, :],
                                cond_ref[b, 1:2, :]).astype(BF16)
        yield
        z_blocks = []
        for j in range(hk // GDN_COL_BLOCK):
            zcols = slice(3 * hk + j * GDN_COL_BLOCK, 3 * hk + (j + 1) * GDN_COL_BLOCK)
            z_blocks.append(jnp.dot(hb, wqkvz_ref[:, zcols], preferred_element_type=F32))
            yield
        ctx["z"] = jnp.concatenate(z_blocks, axis=1)
        heads_per_piece = GDN_COL_BLOCK // HEAD_DIM
        for j in range(3 * hk // GDN_COL_BLOCK):
            cols = slice(j * GDN_COL_BLOCK, (j + 1) * GDN_COL_BLOCK)
            pre = jnp.dot(hb, wqkvz_ref[:, cols], preferred_element_type=F32)
            hist_ref[b, SUBLANES:, cols] = pre
            conv = cw_ref[CONV_WIDTH - 1:CONV_WIDTH, cols] * pre
            for k in range(1, CONV_WIDTH):
                conv = conv + (cw_ref[CONV_WIDTH - 1 - k:CONV_WIDTH - k, cols]
                               * hist_ref[b, SUBLANES - k:SUBLANES - k + tm, cols])
            hist_ref[b, :SUBLANES, cols] = pre[tm - SUBLANES:, :]
            act = conv * (jnp.tanh(conv) + 1.0)
            for i in range(heads_per_piece):
                which, h = divmod(j * heads_per_piece + i, nheads)
                a_h = act[:, i * HEAD_DIM:(i + 1) * HEAD_DIM]
                if which < 2:
                    inv = lax.rsqrt(jnp.sum(a_h * a_h, axis=-1, keepdims=True) + EPS)
                    a_h = a_h * (inv * (HEAD_DIM ** -0.5) if which == 0 else inv)
                dst_ref = (qp_ref, kp_ref, vp_ref)[which]
                hp, par = divmod(h, 2)
                for c in range(nchunks):
                    dst_ref[slot, c, hp, par * c_len:(par + 1) * c_len, :] = (
                        a_h[c * c_len:(c + 1) * c_len, :])
            yield
        ab = jnp.dot(hb, wab_ref[...], preferred_element_type=F32)
        abt = lax.dot_general(wabt_ref[...], hb, (((1,), (1,)), ((), ())),
                              preferred_element_type=F32)
        g_col = -jnp.exp(alog_ref[...]) * _softplus(ab[:, :LANES] + dtb_ref[...])
        g_row = -jnp.exp(alogr_ref[...]) * _softplus(abt + dtbr_ref[...])
        ctx["beta"] = _sigmoid(ab[:, LANES:])
        gcol_parts = _split3(g_col)
        grow_parts = _split3(g_row)
        gc_blocks, gcr_blocks = [], []
        for i in range(tm // LANES):
            sl = slice(i * LANES, (i + 1) * LANES)
            gc_blocks.append(sum(jnp.dot(tri_ref[...], p[sl, :], preferred_element_type=F32)
                                 for p in gcol_parts))
            gcr_blocks.append(sum(jnp.dot(p[:, sl], trit_ref[...], preferred_element_type=F32)
                                  for p in grow_parts))
        ctx["gc"] = jnp.concatenate(gc_blocks, axis=0)
        ctx["gcr"] = jnp.concatenate(gcr_blocks, axis=1)
        yield

    def pair_cols(arr):
        mats = []
        for c in range(nchunks):
            blk_rows = arr[c * c_len:(c + 1) * c_len, :]
            for hp in range(npairs):
                mats.append(jnp.concatenate(
                    [jnp.broadcast_to(blk_rows[:, 2 * hp + par:2 * hp + par + 1], (c_len, HEAD_DIM))
                     for par in range(2)], axis=0))
        return jnp.stack(mats)

    def pair_rows(arr):
        mats = []
        for c in range(nchunks):
            for hp in range(npairs):
                mats.append(jnp.concatenate(
                    [arr[2 * hp + par:2 * hp + par + 1, c * c_len:(c + 1) * c_len]
                     for par in range(2)], axis=1))
        return jnp.stack(mats)

    def merge_level(t_mat, a_mat, m, blk):
        n = 2 * c_len
        if blk < SUBLANES:
            return t_mat - _bmm(t_mat, _bmm(jnp.where(m[None], a_mat, 0.0), t_mat))
        ups = [slice(r, r + blk) for r in range(0, n, 2 * blk)]
        lows = [slice(r, r + blk) for r in range(blk, n, 2 * blk)]
        o_lo = jnp.concatenate([jnp.where(m[sl][None], a_mat[:, sl], 0.0) for sl in lows], axis=1)
        x_lo = _bmm(o_lo, t_mat)
        zeros = jnp.zeros((n_mat, blk, n), F32)
        x_full = jnp.concatenate(
            [piece for i in range(len(lows)) for piece in (zeros, x_lo[:, i * blk:(i + 1) * blk])],
            axis=1)
        upd = _bmm(jnp.concatenate([t_mat[:, sl] for sl in lows], axis=1), x_full)
        return jnp.concatenate(
            [piece for i, (up, lo) in enumerate(zip(ups, lows))
             for piece in (t_mat[:, up], t_mat[:, lo] - upd[:, i * blk:(i + 1) * blk])], axis=1)

    def stage_b(tile, slot, ctx):
        b, s = tile
        rows = pl.ds(s * tm, tm)
        q2 = qp_ref[slot].reshape(n_mat, 2 * c_len, HEAD_DIM)
        k2 = kp_ref[slot].reshape(n_mat, 2 * c_len, HEAD_DIM)
        v2 = vp_ref[slot].reshape(n_mat, 2 * c_len, HEAD_DIM)
        beta2 = pair_cols(ctx["beta"])
        gcol2 = pair_cols(ctx["gc"])
        grow2 = pair_rows(ctx["gcr"])
        yield
        eg2 = jnp.exp(gcol2)
        glast2 = jnp.concatenate(
            [jnp.broadcast_to(gcol2[:, (par + 1) * c_len - 1:(par + 1) * c_len, :],
                              (n_mat, c_len, HEAD_DIM)) for par in range(2)], axis=1)
        kd2 = jnp.exp(glast2 - gcol2)
        decay = jnp.exp(jnp.where(causal[None], gcol2 - grow2, -1e30))
        yield
        kb2 = k2 * beta2
        gram = _bmm_nt(jnp.concatenate([kb2, q2], axis=1), k2)
        a_mat = jnp.where(strict[None], gram[:, :2 * c_len] * decay, 0.0)
        attn = gram[:, 2 * c_len:] * decay
        yield
        t_mat = eye[None] - jnp.where(merge_masks[0][None], a_mat, 0.0)
        for lvl, m in enumerate(merge_masks[1:], start=1):
            t_mat = merge_level(t_mat, a_mat, m, 2 ** lvl)
            yield
        wu = _bmm(t_mat, jnp.concatenate([kb2 * eg2, v2 * beta2], axis=2))
        w2 = wu[:, :, :HEAD_DIM]
        u2 = wu[:, :, HEAD_DIM:]
        qd2 = q2 * eg2
        kdec2 = k2 * kd2
        yield
        st = slice(b * npairs, (b + 1) * npairs)
        o_chunks = []
        for c in range(nchunks):
            sl = slice(c * npairs, (c + 1) * npairs)
            vn_parts, qs_parts = [], []
            for par in range(2):
                rs = slice(par * c_len, (par + 1) * c_len)
                wq = _bmm(jnp.concatenate([w2[sl, rs], qd2[sl, rs]], axis=1), state_ref[st, par])
                vn_parts.append(u2[sl, rs] - wq[:, :c_len])
                qs_parts.append(wq[:, c_len:])
            v_new = jnp.concatenate(vn_parts, axis=1)
            o_chunks.append(jnp.concatenate(qs_parts, axis=1) + _bmm(attn[sl], v_new))
            for par in range(2):
                rs = slice(par * c_len, (par + 1) * c_len)
                e_last = jnp.broadcast_to(eg2[sl, (par + 1) * c_len - 1:(par + 1) * c_len, :],
                                          (npairs, HEAD_DIM, HEAD_DIM))
                state_ref[st, par] = (state_ref[st, par] * e_last
                                      + _bmm_tn(kdec2[sl, rs], v_new[:, rs]))
            yield
        parts = []
        for h in range(nheads):
            hp, par = divmod(h, 2)
            oh = jnp.concatenate([oc[hp, par * c_len:(par + 1) * c_len, :] for oc in o_chunks],
                                 axis=0)
            parts.append(oh * lax.rsqrt(jnp.mean(oh * oh, axis=-1, keepdims=True) + EPS))
        z = ctx["z"]
        og = (jnp.concatenate(parts, axis=1) * og_ref[...] * (z * (jnp.tanh(z) + 1.0))).astype(BF16)
        yield
        xn_blocks = []
        for j in range(d // GDN_COL_BLOCK):
            cols = slice(j * GDN_COL_BLOCK, (j + 1) * GDN_COL_BLOCK)
            out = jnp.dot(og, outw_ref[:, cols], preferred_element_type=F32)
            xn = x_ref[b, rows, cols] + cond_ref[b, 2:3, cols] * out
            if final_norm:
                xn_blocks.append(xn)
            else:
                o_ref[b, rows, cols] = xn
            yield
        if final_norm:
            xn = jnp.concatenate(xn_blocks, axis=1)
            ms = jnp.mean(xn * xn, axis=-1, keepdims=True)
            o_ref[b, rows, :] = xn * lax.rsqrt(ms + EPS) * fg_ref[...]

    tiles = [(b, s) for s in range(ns) for b in range(nb)]
    ctxs = [dict() for _ in tiles]
    order = "abab" + "a" + "ab" + "a" + "ab" * 10 + "aaa"
    _interleave("", a=stage_a(tiles[0], 0, ctxs[0]))
    for idx, tile in enumerate(tiles):
        gens = dict(b=stage_b(tile, idx % 2, ctxs[idx]))
        if idx + 1 < len(tiles):
            gens["a"] = stage_a(tiles[idx + 1], (idx + 1) % 2, ctxs[idx + 1])
        _interleave(order, **gens)


def _pair_tri():
    r = np.arange(LANES)
    m = (r[:, None] >= r[None, :]) & (r[:, None] // GDN_CHUNK == r[None, :] // GDN_CHUNK)
    return m.astype(np.float32)


def _gdn_layer(x, cond_i, norm_g, wqkvz_all, w_ab, conv_w, a_log, dt_bias, onorm_g, out_w_all,
               final_g, j, *, nb, ns, tm, final_norm):
    bsz, s, d = x.shape
    nheads = a_log.shape[0]
    hk = nheads * HEAD_DIM
    nchunks = tm // GDN_CHUNK
    npairs = nheads // 2
    o1 = 3 * hk
    wa = w_ab[:, :nheads]
    wb = w_ab[:, nheads:]
    pad_l = lambda w: jnp.pad(w, ((0, 0), (0, LANES - nheads)))
    wab = jnp.concatenate([pad_l(wa), pad_l(wb)], axis=1).astype(BF16)
    wabt = w_ab.T.astype(BF16)
    lane_vec = lambda v: jnp.pad(v, (0, LANES - nheads)).reshape(1, LANES)
    row_vec = lambda v: jnp.broadcast_to(jnp.pad(v, (0, nheads))[:, None], (2 * nheads, tm))
    tri = _pair_tri()
    const = lambda *shape: pl.BlockSpec(shape, lambda b, t: (0,) * len(shape))
    layer = lambda *shape: pl.BlockSpec((None,) + shape, lambda b, t: (j,) + (0,) * len(shape))
    kern = functools.partial(_gdn_kernel, nb=nb, ns=ns, tm=tm, d=d, nheads=nheads,
                             final_norm=final_norm)
    pair_stack = pltpu.VMEM((2, nchunks, npairs, 2 * GDN_CHUNK, HEAD_DIM), F32)
    return pl.pallas_call(
        kern,
        grid=(bsz // nb, s // (ns * tm)),
        in_specs=[pl.BlockSpec((nb, ns * tm, d), lambda b, t: (b, t, 0)),
                  pl.BlockSpec((nb, 3, d), lambda b, t: (b, 0, 0)),
                  const(1, d), layer(d, wqkvz_all.shape[2]), const(d, 2 * LANES),
                  const(2 * nheads, d), const(CONV_WIDTH, o1), const(1, LANES), const(1, LANES),
                  const(2 * nheads, tm), const(2 * nheads, tm), const(LANES, LANES),
                  const(LANES, LANES), const(1, hk), layer(hk, d), const(1, d)],
        out_specs=pl.BlockSpec((nb, ns * tm, d), lambda b, t: (b, t, 0)),
        out_shape=jax.ShapeDtypeStruct((bsz, s, d), F32),
        scratch_shapes=[pltpu.VMEM((nb, SUBLANES + tm, o1), F32),
                        pltpu.VMEM((nb * npairs, 2, HEAD_DIM, HEAD_DIM), F32),
                        pair_stack, pair_stack, pair_stack],
        compiler_params=pltpu.CompilerParams(dimension_semantics=("arbitrary", "arbitrary"),
                                             vmem_limit_bytes=VMEM_LIMIT_BYTES),
        name="gdn_layer",
    )(x, cond_i, norm_g.reshape(1, d), wqkvz_all, wab, wabt, conv_w, lane_vec(a_log),
      lane_vec(dt_bias), row_vec(a_log), row_vec(dt_bias), jnp.asarray(tri, BF16),
      jnp.asarray(tri.T, BF16), jnp.tile(onorm_g, nheads).reshape(1, hk), out_w_all,
      final_g.reshape(1, d))


def kernel(x, c, ada_w, ada_b, norm_g, final_g, lru_in_w, lru_conv_w, lru_conv_b, lru_gate_w,
           lru_gate_b, lru_lambda, lru_out_w, gdn_in_w, gdn_conv_w, gdn_a_log, gdn_dt_bias,
           gdn_onorm_g, gdn_out_w):
    depth = ada_w.shape[0]
    assert depth % 2 == 0, "layers alternate RG-LRU / DeltaNet and the last one applies the final norm"
    cond = _cond(c, ada_w, ada_b)
    d = x.shape[-1]
    n_hv = gdn_a_log.shape[1] * HEAD_DIM
    n_qkv = 3 * n_hv
    halve_tail = lambda n_keep, n_all: jnp.where(jnp.arange(n_all) < n_keep, 1.0, 0.5).astype(F32)
    lru_in_bf = (lru_in_w * halve_tail(d, lru_in_w.shape[-1])).astype(BF16)
    lru_out_bf = lru_out_w.astype(BF16)
    lru_gw_bf = (0.5 * jnp.concatenate([lru_gate_w[:, 0], lru_gate_w[:, 1]], axis=-1)).astype(BF16)
    gdn_in_bf = (gdn_in_w[:, :, :n_qkv + n_hv] * halve_tail(n_qkv, n_qkv + n_hv)).astype(BF16)
    gdn_out_bf = gdn_out_w.astype(BF16)
    for i in range(depth):
        j = i // 2
        if i % 2 == 0:
            x = _lru_layer(x, cond[i], norm_g[i], lru_in_bf, lru_conv_w[j], lru_conv_b[j], lru_gw_bf,
                           0.5 * lru_gate_b[j], lru_lambda[j], lru_out_bf, j, nb=1, ns=4, tm=256)
        else:
            x = _gdn_layer(x, cond[i], norm_g[i], gdn_in_bf, gdn_in_w[j, :, n_qkv + n_hv:],
                           0.5 * gdn_conv_w[j], gdn_a_log[j], gdn_dt_bias[j], gdn_onorm_g[j],
                           gdn_out_bf, final_g, j, nb=1, ns=2, tm=256, final_norm=(i == depth - 1))
    return x
```

```python
import functools

import numpy as np
import jax
import jax.numpy as jnp
from jax import lax
from jax.experimental import pallas as pl
from jax.experimental.pallas import tpu as pltpu

F32 = jnp.float32
BF16 = jnp.bfloat16

EPS = 1e-6
RG_C = 8.0
CONV_WIDTH = 4
SUBLANES = 8
LANES = 128
GDN_CHUNK = 64
HEAD_DIM = 128
LRU_BLOCK_W = 128
LRU_COL_BLOCK = 256
GDN_COL_BLOCK = 256
VMEM_LIMIT_BYTES = 56 * 1024 * 1024


def _sigmoid(v):
    return 0.5 * jnp.tanh(0.5 * v) + 0.5


def _softplus(v):
    return jnp.maximum(v, 0.0) + jnp.log1p(jnp.exp(-jnp.abs(v)))


def _bdot(a, b):
    return jnp.dot(a.astype(BF16), b.astype(BF16), preferred_element_type=F32)


def _bmm(a, b):
    return lax.dot_general(a.astype(BF16), b.astype(BF16), (((2,), (1,)), ((0,), (0,))),
                           preferred_element_type=F32)


def _bmm_nt(a, b):
    return lax.dot_general(a.astype(BF16), b.astype(BF16), (((2,), (2,)), ((0,), (0,))),
                           preferred_element_type=F32)


def _bmm_tn(a, b):
    return lax.dot_general(a.astype(BF16), b.astype(BF16), (((1,), (1,)), ((0,), (0,))),
                           preferred_element_type=F32)


def _split3(v):
    v1 = v.astype(BF16)
    r1 = v - v1.astype(F32)
    v2 = r1.astype(BF16)
    v3 = (r1 - v2.astype(F32)).astype(BF16)
    return v1, v2, v3


def _modulated_rmsnorm(x, g, shift, scale):
    ms = jnp.mean(x * x, axis=-1, keepdims=True)
    return (x * lax.rsqrt(ms + EPS)) * (g * (1.0 + scale)) + shift


def _interleave(order, **gens):
    def step(name):
        try:
            next(gens[name])
        except StopIteration:
            gens.pop(name)

    for name in order:
        if name in gens:
            step(name)
    while gens:
        for name in list(gens):
            step(name)


def _cond_kernel(c_ref, w_ref, b_ref, o_ref):
    c = c_ref[...]
    ca = c * _sigmoid(c)
    o_ref[0] = _bdot(ca, w_ref[0]) + b_ref[0]


def _cond(c, ada_w, ada_b):
    depth, d, d3 = ada_w.shape
    bsz = c.shape[0]
    bp = -(-bsz // SUBLANES) * SUBLANES
    c8 = jnp.pad(c, ((0, bp - bsz), (0, 0)))
    tn = d
    out = pl.pallas_call(
        _cond_kernel,
        grid=(depth, d3 // tn),
        in_specs=[pl.BlockSpec((bp, d), lambda i, j: (0, 0)),
                  pl.BlockSpec((1, d, tn), lambda i, j: (i, 0, j)),
                  pl.BlockSpec((1, 1, tn), lambda i, j: (i, 0, j))],
        out_specs=pl.BlockSpec((1, bp, tn), lambda i, j: (i, 0, j)),
        out_shape=jax.ShapeDtypeStruct((depth, bp, d3), F32),
        name="adaln_cond",
    )(c8, ada_w, ada_b.reshape(depth, 1, d3))
    return out[:, :bsz].reshape(depth, bsz, 3, d)


def _lru_kernel(x_ref, cond_ref, ng_ref, inw_ref, cw_ref, cb_ref, gw_ref,
                gb_ref, lam_ref, outw_ref, o_ref, tail_ref, carry_ref, *, nb, ns, tm, d):
    t = pl.program_id(1)

    @pl.when(t == 0)
    def _():
        tail_ref[...] = jnp.zeros_like(tail_ref)
        carry_ref[...] = jnp.zeros_like(carry_ref)

    ncol = d // LRU_COL_BLOCK
    tiles = [(b, s) for s in range(ns) for b in range(nb)]

    def permuted_input(b, s):
        x = x_ref[b, pl.ds(s * tm, tm), :]
        h = _modulated_rmsnorm(x, ng_ref[...], cond_ref[b, 0:1, :], cond_ref[b, 1:2, :])
        return pltpu.einshape("jid->ijd", h.reshape(SUBLANES, tm // SUBLANES, d)).reshape(
            tm, d).astype(BF16)

    def in_proj(hp, n):
        cols = slice(n * LRU_COL_BLOCK, (n + 1) * LRU_COL_BLOCK)
        zcols = slice(d + n * LRU_COL_BLOCK, d + (n + 1) * LRU_COL_BLOCK)
        return (jnp.dot(hp, inw_ref[:, cols], preferred_element_type=F32),
                jnp.dot(hp, inw_ref[:, zcols], preferred_element_type=F32))

    def out_proj(tile, yn, n):
        b, s = tile
        rows = pl.ds(s * tm, tm)
        cols = slice(n * LRU_COL_BLOCK, (n + 1) * LRU_COL_BLOCK)
        out = jnp.dot(yn, outw_ref[:, cols], preferred_element_type=F32)
        o_ref[b, rows, cols] = x_ref[b, rows, cols] + cond_ref[b, 2:3, cols] * out

    def unpermute(y_blocks):
        y = jnp.concatenate(y_blocks, axis=1)
        return pltpu.einshape("ijd->jid", y.reshape(tm // SUBLANES, SUBLANES, d)).reshape(
            tm, d).astype(BF16)

    hp = permuted_input(*tiles[0])
    proj = [in_proj(hp, n) for n in range(ncol)]
    prev, yn_prev = None, None
    for idx, (b, s) in enumerate(tiles):
        nxt = tiles[idx + 1] if idx + 1 < len(tiles) else None
        if nxt is not None:
            hp = permuted_input(*nxt)
        proj_next, y_blocks = [], []
        def recurrence(n, half):
            lo = half * LRU_BLOCK_W
            cols = slice(n * LRU_COL_BLOCK + lo, n * LRU_COL_BLOCK + lo + LRU_BLOCK_W)
            blk = n * (LRU_COL_BLOCK // LRU_BLOCK_W) + half
            y_blocks.append(_lru_recurrence_block(
                proj[n][0][:, lo:lo + LRU_BLOCK_W], proj[n][1][:, lo:lo + LRU_BLOCK_W],
                cw_ref.at[:, cols], cb_ref.at[:, cols], gw_ref.at[pl.ds(blk, 1)],
                gb_ref.at[:, cols], lam_ref.at[:, cols], tail_ref.at[b, :, cols],
                carry_ref.at[b, :, cols], tm=tm))

        for n in range(ncol):
            if nxt is not None:
                proj_next.append(in_proj(hp, n))
            recurrence(n, 0)
            if prev is not None:
                out_proj(prev, yn_prev, n)
            recurrence(n, 1)
        proj = proj_next
        prev, yn_prev = (b, s), unpermute(y_blocks)
    for n in range(ncol):
        out_proj(prev, yn_prev, n)


def _lru_recurrence_block(xb, zg, cw_ref, cb_ref, gw_ref, gb_ref, lam_ref, tail_ref, carry_ref, *, tm):
    g_rows = tm // SUBLANES
    w = xb.shape[1]
    n_tail = (CONV_WIDTH - 1) * SUBLANES
    cur_tail = xb[tm - n_tail:, :]
    prev_tail = tail_ref[...]
    sub = lax.broadcasted_iota(jnp.int32, (SUBLANES, w), 0)
    heads = []
    for m in range(CONV_WIDTH - 1):
        cur = pltpu.roll(cur_tail[m * SUBLANES:(m + 1) * SUBLANES, :], 1, 0)
        prv = pltpu.roll(prev_tail[m * SUBLANES:(m + 1) * SUBLANES, :], 1, 0)
        heads.append(jnp.where(sub == 0, prv, cur))
    tail_ref[...] = cur_tail
    ext = jnp.concatenate(heads + [xb], axis=0)
    xf = cb_ref[...]
    for k in range(CONV_WIDTH):
        off = n_tail - k * SUBLANES
        xf = xf + cw_ref[CONV_WIDTH - 1 - k:CONV_WIDTH - k, :] * ext[off:off + tm, :]

    xfb = xf.astype(BF16)
    pre_r, pre_i = [], []
    for n in range(w // LRU_BLOCK_W):
        pre = jnp.dot(xfb[:, n * LRU_BLOCK_W:(n + 1) * LRU_BLOCK_W], gw_ref[n],
                      preferred_element_type=F32)
        pre_r.append(pre[:, :LRU_BLOCK_W])
        pre_i.append(pre[:, LRU_BLOCK_W:])
    tanh_r = jnp.tanh(jnp.concatenate(pre_r, axis=1) + gb_ref[0:1, :])
    tanh_i = jnp.tanh(jnp.concatenate(pre_i, axis=1) + gb_ref[1:2, :])
    half_c = (-0.5 * RG_C) * _softplus(-lam_ref[...])
    log_a = half_c * tanh_r + half_c
    a_t = jnp.exp(log_a)
    th = jnp.tanh(log_a)
    msq = -0.5 * th / (1.0 - th)
    half_mult = jnp.where(msq > 0.0, msq * lax.rsqrt(msq), 0.0)
    b_t = (half_mult * xf) * (tanh_i + 1.0)

    h_loc = jnp.zeros((SUBLANES, w), F32)
    p_loc = jnp.ones((SUBLANES, w), F32)
    h_rows, p_rows = [], []
    for i in range(g_rows):
        a_i = a_t[i * SUBLANES:(i + 1) * SUBLANES, :]
        h_loc = a_i * h_loc + b_t[i * SUBLANES:(i + 1) * SUBLANES, :]
        p_loc = a_i * p_loc
        h_rows.append(h_loc)
        p_rows.append(p_loc)
    c_in = jnp.broadcast_to(carry_ref[SUBLANES - 1:SUBLANES, :], (SUBLANES, w))
    c_sub = c_in
    for _ in range(SUBLANES - 1):
        c_sub = jnp.where(sub == 0, c_in, pltpu.roll(h_loc + p_loc * c_sub, 1, 0))
    carry_ref[...] = h_loc + p_loc * c_sub
    hseq = jnp.concatenate([hr + pr * c_sub for hr, pr in zip(h_rows, p_rows)], axis=0)
    return (hseq * zg) * (jnp.tanh(zg) + 1.0)


def _lru_layer(x, cond_i, norm_g, in_w_all, conv_w, conv_b, gw_all, gate_b, lam, out_w_all, j, *, nb,
               ns, tm):
    bsz, s, d = x.shape
    nblk = d // LRU_BLOCK_W
    const = lambda *shape: pl.BlockSpec(shape, lambda b, t: (0,) * len(shape))
    layer = lambda *shape: pl.BlockSpec((None,) + shape, lambda b, t: (j,) + (0,) * len(shape))
    kern = functools.partial(_lru_kernel, nb=nb, ns=ns, tm=tm, d=d)
    return pl.pallas_call(
        kern,
        grid=(bsz // nb, s // (ns * tm)),
        in_specs=[pl.BlockSpec((nb, ns * tm, d), lambda b, t: (b, t, 0)),
                  pl.BlockSpec((nb, 3, d), lambda b, t: (b, 0, 0)),
                  const(1, d), layer(d, 2 * d), const(CONV_WIDTH, d),
                  const(1, d), layer(nblk, LRU_BLOCK_W, 2 * LRU_BLOCK_W), const(2, d), const(1, d),
                  layer(d, d)],
        out_specs=pl.BlockSpec((nb, ns * tm, d), lambda b, t: (b, t, 0)),
        out_shape=jax.ShapeDtypeStruct((bsz, s, d), F32),
        scratch_shapes=[pltpu.VMEM((nb, (CONV_WIDTH - 1) * SUBLANES, d), F32),
                        pltpu.VMEM((nb, SUBLANES, d), F32)],
        compiler_params=pltpu.CompilerParams(dimension_semantics=("arbitrary", "arbitrary"),
                                             vmem_limit_bytes=VMEM_LIMIT_BYTES),
        name="rglru_layer",
    )(x, cond_i, norm_g.reshape(1, d), in_w_all, conv_w, conv_b.reshape(1, d), gw_all, gate_b,
      lam.reshape(1, d), out_w_all)


def _gdn_kernel(x_ref, cond_ref, ng_ref, wqkvz_ref, wab_ref, wabt_ref, cw_ref, alog_ref,
                dtb_ref, alogr_ref, dtbr_ref, tri_ref, trit_ref, og_ref, outw_ref, fg_ref, o_ref,
                hist_ref, state_ref, qp_ref, kp_ref, vp_ref,
                *, nb, ns, tm, d, nheads, final_norm):
    c_len = GDN_CHUNK
    nchunks = tm // c_len
    npairs = nheads // 2
    n_mat = nchunks * npairs
    hk = nheads * HEAD_DIM
    n_tail = (CONV_WIDTH - 1) * SUBLANES
    t = pl.program_id(1)

    @pl.when(t == 0)
    def _():
        hist_ref[...] = jnp.zeros_like(hist_ref)
        state_ref[...] = jnp.zeros_like(state_ref)

    rr = lax.broadcasted_iota(jnp.int32, (2 * c_len, 2 * c_len), 0)
    cc = lax.broadcasted_iota(jnp.int32, (2 * c_len, 2 * c_len), 1)
    tr = _chunk_time(rr & (c_len - 1))
    tc = _chunk_time(cc & (c_len - 1))
    same = (rr >= c_len) == (cc >= c_len)
    causal = same & (tr >= tc)
    strict = same & (tr > tc)
    eye = (rr == cc).astype(F32)
    merge_masks = []
    blk = 1
    while blk < c_len:
        merge_masks.append(same & ((tr ^ tc) < 2 * blk) & ((tr & blk) != 0) & ((tc & blk) == 0))
        blk *= 2
    sub = lax.broadcasted_iota(jnp.int32, (SUBLANES, GDN_COL_BLOCK), 0)

    def stage_a(tile, slot, ctx):
        b, s = tile
        rows = pl.ds(s * tm, tm)
        h = _modulated_rmsnorm(x_ref[b, rows, :], ng_ref[...], cond_ref[b, 0:1, :],
                               cond_ref[b, 1:2, :])
        hb = pltpu.einshape("cijd->cjid", h.reshape(nchunks, SUBLANES, SUBLANES, d)).reshape(
            tm, d).astype(BF16)
        yield
        z_blocks = []
        for j in range(hk // GDN_COL_BLOCK):
            zcols = slice(3 * hk + j * GDN_COL_BLOCK, 3 * hk + (j + 1) * GDN_COL_BLOCK)
            z_blocks.append(jnp.dot(hb, wqkvz_ref[:, zcols], preferred_element_type=F32))
            yield
        ctx["z"] = jnp.concatenate(z_blocks, axis=1)
        heads_per_piece = GDN_COL_BLOCK // HEAD_DIM
        for j in range(3 * hk // GDN_COL_BLOCK):
            cols = slice(j * GDN_COL_BLOCK, (j + 1) * GDN_COL_BLOCK)
            pre = jnp.dot(hb, wqkvz_ref[:, cols], preferred_element_type=F32)
            prev_tail = hist_ref[b, :, cols]
            conv_chunks = []
            for c in range(nchunks):
                cur = pre[c * c_len:(c + 1) * c_len, :]
                cur_tail = cur[c_len - n_tail:, :]
                heads = []
                for m in range(CONV_WIDTH - 1):
                    rows8 = slice(m * SUBLANES, (m + 1) * SUBLANES)
                    heads.append(jnp.where(sub == 0, pltpu.roll(prev_tail[rows8, :], 1, 0),
                                           pltpu.roll(cur_tail[rows8, :], 1, 0)))
                ext = jnp.concatenate(heads + [cur], axis=0)
                conv_c = cw_ref[CONV_WIDTH - 1:CONV_WIDTH, cols] * cur
                for k in range(1, CONV_WIDTH):
                    off = n_tail - k * SUBLANES
                    conv_c = conv_c + (cw_ref[CONV_WIDTH - 1 - k:CONV_WIDTH - k, cols]
                                       * ext[off:off + c_len, :])
                conv_chunks.append(conv_c)
                prev_tail = cur_tail
            hist_ref[b, :, cols] = prev_tail
            conv = jnp.concatenate(conv_chunks, axis=0)
            act = conv * (jnp.tanh(conv) + 1.0)
            for i in range(heads_per_piece):
                which, h = divmod(j * heads_per_piece + i, nheads)
                a_h = act[:, i * HEAD_DIM:(i + 1) * HEAD_DIM]
                if which < 2:
                    inv = lax.rsqrt(jnp.sum(a_h * a_h, axis=-1, keepdims=True) + EPS)
                    a_h = a_h * (inv * (HEAD_DIM ** -0.5) if which == 0 else inv)
                dst_ref = (qp_ref, kp_ref, vp_ref)[which]
                hp, par = divmod(h, 2)
                for c in range(nchunks):
                    dst_ref[slot, c, hp, par * c_len:(par + 1) * c_len, :] = (
                        a_h[c * c_len:(c + 1) * c_len, :])
            yield
        ab = jnp.dot(hb, wab_ref[...], preferred_element_type=F32)
        abt = lax.dot_general(wabt_ref[...], hb, (((1,), (1,)), ((), ())),
                              preferred_element_type=F32)
        g_col = -jnp.exp(alog_ref[...]) * _softplus(ab + dtb_ref[...])
        g_row = -jnp.exp(alogr_ref[...]) * _softplus(abt + dtbr_ref[...])
        ctx["beta"] = _sigmoid(ab)
        gcol_parts = _split3(g_col)
        grow_parts = _split3(g_row)
        gc_blocks, gcr_blocks = [], []
        for i in range(tm // LANES):
            sl = slice(i * LANES, (i + 1) * LANES)
            gc_blocks.append(sum(jnp.dot(tri_ref[...], p[sl, :], preferred_element_type=F32)
                                 for p in gcol_parts))
            gcr_blocks.append(sum(jnp.dot(p[:, sl], trit_ref[...], preferred_element_type=F32)
                                  for p in grow_parts))
        ctx["gc"] = jnp.concatenate(gc_blocks, axis=0)
        ctx["gcr"] = jnp.concatenate(gcr_blocks, axis=1)
        yield

    def pair_cols(arr, lane0=0):
        mats = []
        for c in range(nchunks):
            blk_rows = arr[c * c_len:(c + 1) * c_len, :]
            for hp in range(npairs):
                lanes = [lane0 + 2 * hp + par for par in range(2)]
                mats.append(jnp.concatenate(
                    [jnp.broadcast_to(blk_rows[:, ln:ln + 1], (c_len, HEAD_DIM)) for ln in lanes],
                    axis=0))
        return jnp.stack(mats)

    def pair_rows(arr):
        mats = []
        for c in range(nchunks):
            for hp in range(npairs):
                mats.append(jnp.concatenate(
                    [arr[2 * hp + par:2 * hp + par + 1, c * c_len:(c + 1) * c_len]
                     for par in range(2)], axis=1))
        return jnp.stack(mats)

    def merge_level(t_mat, a_mat, m, blk):
        n = 2 * c_len
        if blk >= SUBLANES:
            return t_mat - _bmm(t_mat, _bmm(jnp.where(m[None], a_mat, 0.0), t_mat))
        vrows = [slice(v * SUBLANES, (v + 1) * SUBLANES) for v in range(n // SUBLANES)]
        late = [(v & blk) != 0 for v in range(n // SUBLANES)]
        pick = lambda arr: jnp.concatenate([arr[:, sl] for sl, lt in zip(vrows, late) if lt], axis=1)
        o_lo = jnp.where(pick(m[None]), pick(a_mat), 0.0)
        x_lo = _bmm(o_lo, t_mat)
        zeros = jnp.zeros((n_mat, SUBLANES, n), F32)
        pieces, idx = [], 0
        for lt in late:
            pieces.append(x_lo[:, idx * SUBLANES:(idx + 1) * SUBLANES] if lt else zeros)
            idx += lt
        upd = _bmm(pick(t_mat), jnp.concatenate(pieces, axis=1))
        pieces, idx = [], 0
        for sl, lt in zip(vrows, late):
            pieces.append(t_mat[:, sl] - upd[:, idx * SUBLANES:(idx + 1) * SUBLANES] if lt
                          else t_mat[:, sl])
            idx += lt
        return jnp.concatenate(pieces, axis=1)

    def stage_b(tile, slot, ctx):
        b, s = tile
        rows = pl.ds(s * tm, tm)
        q2 = qp_ref[slot].reshape(n_mat, 2 * c_len, HEAD_DIM)
        k2 = kp_ref[slot].reshape(n_mat, 2 * c_len, HEAD_DIM)
        v2 = vp_ref[slot].reshape(n_mat, 2 * c_len, HEAD_DIM)
        beta2 = pair_cols(ctx["beta"], lane0=nheads)
        gcol2 = pair_cols(ctx["gc"])
        grow2 = pair_rows(ctx["gcr"])
        yield
        eg2 = jnp.exp(gcol2)
        glast2 = jnp.concatenate(
            [jnp.broadcast_to(gcol2[:, (par + 1) * c_len - 1:(par + 1) * c_len, :],
                              (n_mat, c_len, HEAD_DIM)) for par in range(2)], axis=1)
        kd2 = jnp.exp(glast2 - gcol2)
        decay = jnp.exp(jnp.where(causal[None], gcol2 - grow2, -1e30))
        yield
        kb2 = k2 * beta2
        gram = _bmm_nt(jnp.concatenate([kb2, q2], axis=1), k2)
        a_mat = jnp.where(strict[None], gram[:, :2 * c_len] * decay, 0.0)
        attn = gram[:, 2 * c_len:] * decay
        yield
        t_mat = eye[None] - jnp.where(merge_masks[0][None], a_mat, 0.0)
        for lvl, m in enumerate(merge_masks[1:], start=1):
            t_mat = merge_level(t_mat, a_mat, m, 2 ** lvl)
            yield
        wu = _bmm(t_mat, jnp.concatenate([kb2 * eg2, v2 * beta2], axis=2))
        w2 = wu[:, :, :HEAD_DIM]
        u2 = wu[:, :, HEAD_DIM:]
        qd2 = q2 * eg2
        kdec2 = k2 * kd2
        yield
        st = slice(b * npairs, (b + 1) * npairs)
        o_chunks = []
        for c in range(nchunks):
            sl = slice(c * npairs, (c + 1) * npairs)
            vn_parts, qs_parts = [], []
            for par in range(2):
                rs = slice(par * c_len, (par + 1) * c_len)
                wq = _bmm(jnp.concatenate([w2[sl, rs], qd2[sl, rs]], axis=1), state_ref[st, par])
                vn_parts.append(u2[sl, rs] - wq[:, :c_len])
                qs_parts.append(wq[:, c_len:])
            v_new = jnp.concatenate(vn_parts, axis=1)
            o_chunks.append(jnp.concatenate(qs_parts, axis=1) + _bmm(attn[sl], v_new))
            for par in range(2):
                rs = slice(par * c_len, (par + 1) * c_len)
                e_last = jnp.broadcast_to(eg2[sl, (par + 1) * c_len - 1:(par + 1) * c_len, :],
                                          (npairs, HEAD_DIM, HEAD_DIM))
                state_ref[st, par] = (state_ref[st, par] * e_last
                                      + _bmm_tn(kdec2[sl, rs], v_new[:, rs]))
            yield
        parts = []
        for h in range(nheads):
            hp, par = divmod(h, 2)
            oh = jnp.concatenate([oc[hp, par * c_len:(par + 1) * c_len, :] for oc in o_chunks],
                                 axis=0)
            parts.append(oh * lax.rsqrt(jnp.mean(oh * oh, axis=-1, keepdims=True) + EPS))
        z = ctx["z"]
        og = jnp.concatenate(parts, axis=1) * og_ref[...] * (z * (jnp.tanh(z) + 1.0))
        og = pltpu.einshape("cijd->cjid", og.reshape(nchunks, SUBLANES, SUBLANES, hk)).reshape(
            tm, hk).astype(BF16)
        yield
        xn_blocks = []
        for j in range(d // GDN_COL_BLOCK):
            cols = slice(j * GDN_COL_BLOCK, (j + 1) * GDN_COL_BLOCK)
            out = jnp.dot(og, outw_ref[:, cols], preferred_element_type=F32)
            xn = x_ref[b, rows, cols] + cond_ref[b, 2:3, cols] * out
            if final_norm:
                xn_blocks.append(xn)
            else:
                o_ref[b, rows, cols] = xn
            yield
        if final_norm:
            xn = jnp.concatenate(xn_blocks, axis=1)
            ms = jnp.mean(xn * xn, axis=-1, keepdims=True)
            o_ref[b, rows, :] = xn * lax.rsqrt(ms + EPS) * fg_ref[...]

    tiles = [(b, s) for s in range(ns) for b in range(nb)]
    ctxs = [dict() for _ in tiles]
    order = "abab" + "a" + "ab" + "a" + "ab" * 10 + "aaa"
    _interleave("", a=stage_a(tiles[0], 0, ctxs[0]))
    for idx, tile in enumerate(tiles):
        gens = dict(b=stage_b(tile, idx % 2, ctxs[idx]))
        if idx + 1 < len(tiles):
            gens["a"] = stage_a(tiles[idx + 1], (idx + 1) % 2, ctxs[idx + 1])
        _interleave(order, **gens)


def _chunk_time(p):
    assert SUBLANES == 8 and GDN_CHUNK == 64
    return ((p & 7) << 3) | (p >> 3)


def _pair_tri():
    r = np.arange(LANES)
    tt = _chunk_time(r % GDN_CHUNK)
    m = (tt[:, None] >= tt[None, :]) & (r[:, None] // GDN_CHUNK == r[None, :] // GDN_CHUNK)
    return m.astype(np.float32)


def _gdn_layer(x, cond_i, norm_g, wqkvz_all, w_ab, conv_w, a_log, dt_bias, onorm_g, out_w_all,
               final_g, j, *, nb, ns, tm, final_norm):
    bsz, s, d = x.shape
    nheads = a_log.shape[0]
    hk = nheads * HEAD_DIM
    nchunks = tm // GDN_CHUNK
    npairs = nheads // 2
    o1 = 3 * hk
    wab = jnp.pad(w_ab, ((0, 0), (0, LANES - 2 * nheads))).astype(BF16)
    wabt = w_ab.T.astype(BF16)
    lane_vec = lambda v: jnp.pad(v, (0, LANES - nheads)).reshape(1, LANES)
    row_vec = lambda v: jnp.broadcast_to(jnp.pad(v, (0, nheads))[:, None], (2 * nheads, tm))
    tri = _pair_tri()
    const = lambda *shape: pl.BlockSpec(shape, lambda b, t: (0,) * len(shape))
    layer = lambda *shape: pl.BlockSpec((None,) + shape, lambda b, t: (j,) + (0,) * len(shape))
    kern = functools.partial(_gdn_kernel, nb=nb, ns=ns, tm=tm, d=d, nheads=nheads,
                             final_norm=final_norm)
    pair_stack = pltpu.VMEM((2, nchunks, npairs, 2 * GDN_CHUNK, HEAD_DIM), F32)
    return pl.pallas_call(
        kern,
        grid=(bsz // nb, s // (ns * tm)),
        in_specs=[pl.BlockSpec((nb, ns * tm, d), lambda b, t: (b, t, 0)),
                  pl.BlockSpec((nb, 3, d), lambda b, t: (b, 0, 0)),
                  const(1, d), layer(d, wqkvz_all.shape[2]), const(d, LANES),
                  const(2 * nheads, d), const(CONV_WIDTH, o1), const(1, LANES), const(1, LANES),
                  const(2 * nheads, tm), const(2 * nheads, tm), const(LANES, LANES),
                  const(LANES, LANES), const(1, hk), layer(hk, d), const(1, d)],
        out_specs=pl.BlockSpec((nb, ns * tm, d), lambda b, t: (b, t, 0)),
        out_shape=jax.ShapeDtypeStruct((bsz, s, d), F32),
        scratch_shapes=[pltpu.VMEM((nb, (CONV_WIDTH - 1) * SUBLANES, o1), F32),
                        pltpu.VMEM((nb * npairs, 2, HEAD_DIM, HEAD_DIM), F32),
                        pair_stack, pair_stack, pair_stack],
        compiler_params=pltpu.CompilerParams(dimension_semantics=("arbitrary", "arbitrary"),
                                             vmem_limit_bytes=VMEM_LIMIT_BYTES),
        name="gdn_layer",
    )(x, cond_i, norm_g.reshape(1, d), wqkvz_all, wab, wabt, conv_w, lane_vec(a_log),
      lane_vec(dt_bias), row_vec(a_log), row_vec(dt_bias), jnp.asarray(tri, BF16),
      jnp.asarray(tri.T, BF16), jnp.tile(onorm_g, nheads).reshape(1, hk), out_w_all,
      final_g.reshape(1, d))


def kernel(x, c, ada_w, ada_b, norm_g, final_g, lru_in_w, lru_conv_w, lru_conv_b, lru_gate_w,
           lru_gate_b, lru_lambda, lru_out_w, gdn_in_w, gdn_conv_w, gdn_a_log, gdn_dt_bias,
           gdn_onorm_g, gdn_out_w):
    depth = ada_w.shape[0]
    assert depth % 2 == 0, "layers alternate RG-LRU / DeltaNet and the last one applies the final norm"
    cond = _cond(c, ada_w, ada_b)
    d = x.shape[-1]
    n_hv = gdn_a_log.shape[1] * HEAD_DIM
    n_qkv = 3 * n_hv
    halve_tail = lambda n_keep, n_all: jnp.where(jnp.arange(n_all) < n_keep, 1.0, 0.5).astype(F32)
    lru_in_bf = (lru_in_w * halve_tail(d, lru_in_w.shape[-1])).astype(BF16)
    lru_out_bf = lru_out_w.astype(BF16)
    lru_gw_bf = (0.5 * jnp.concatenate([lru_gate_w[:, 0], lru_gate_w[:, 1]], axis=-1)).astype(BF16)
    z_scale = jnp.where((jnp.arange(gdn_in_w.shape[-1]) >= n_qkv)
                        & (jnp.arange(gdn_in_w.shape[-1]) < n_qkv + n_hv), 0.5, 1.0).astype(F32)
    gdn_in_bf = (gdn_in_w * z_scale).astype(BF16)
    gdn_out_bf = gdn_out_w.astype(BF16)
    for i in range(depth):
        j = i // 2
        if i % 2 == 0:
            x = _lru_layer(x, cond[i], norm_g[i], lru_in_bf, lru_conv_w[j], lru_conv_b[j], lru_gw_bf,
                           0.5 * lru_gate_b[j], lru_lambda[j], lru_out_bf, j, nb=1, ns=4, tm=256)
        else:
            x = _gdn_layer(x, cond[i], norm_g[i], gdn_in_bf, gdn_in_w[j, :, n_qkv + n_hv:],
                           0.5 * gdn_conv_w[j], gdn_a_log[j], gdn_dt_bias[j], gdn_onorm_g[j],
                           gdn_out_bf, final_g, j, nb=1, ns=2, tm=256, final_norm=(i == depth - 1))
    return x
```

```python
import functools

import numpy as np
import jax
import jax.numpy as jnp
from jax import lax
from jax.experimental import pallas as pl
from jax.experimental.pallas import tpu as pltpu

F32 = jnp.float32
BF16 = jnp.bfloat16

EPS = 1e-6
RG_C = 8.0
CONV_WIDTH = 4
SUBLANES = 8
LANES = 128
GDN_CHUNK = 64
HEAD_DIM = 128
LRU_BLOCK_W = 128
LRU_COL_BLOCK = 256
GDN_COL_BLOCK = 256
VMEM_LIMIT_BYTES = 56 * 1024 * 1024


def _sigmoid(v):
    return 0.5 * jnp.tanh(0.5 * v) + 0.5


def _softplus(v):
    return jnp.maximum(v, 0.0) + jnp.log1p(jnp.exp(-jnp.abs(v)))


def _bdot(a, b):
    return jnp.dot(a.astype(BF16), b.astype(BF16), preferred_element_type=F32)


def _bmm(a, b):
    return lax.dot_general(a.astype(BF16), b.astype(BF16), (((2,), (1,)), ((0,), (0,))),
                           preferred_element_type=F32)


def _bmm_nt(a, b):
    return lax.dot_general(a.astype(BF16), b.astype(BF16), (((2,), (2,)), ((0,), (0,))),
                           preferred_element_type=F32)


def _bmm_tn(a, b):
    return lax.dot_general(a.astype(BF16), b.astype(BF16), (((1,), (1,)), ((0,), (0,))),
                           preferred_element_type=F32)


def _split3(v):
    v1 = v.astype(BF16)
    r1 = v - v1.astype(F32)
    v2 = r1.astype(BF16)
    v3 = (r1 - v2.astype(F32)).astype(BF16)
    return v1, v2, v3


def _modulated_rmsnorm(x, g, shift, scale):
    ms = jnp.mean(x * x, axis=-1, keepdims=True)
    return (x * lax.rsqrt(ms + EPS)) * (g * (1.0 + scale)) + shift


def _interleave(order, **gens):
    def step(name):
        try:
            next(gens[name])
        except StopIteration:
            gens.pop(name)

    for name in order:
        if name in gens:
            step(name)
    while gens:
        for name in list(gens):
            step(name)


def _cond_kernel(c_ref, w_ref, b_ref, o_ref):
    c = c_ref[...]
    ca = c * _sigmoid(c)
    o_ref[0] = _bdot(ca, w_ref[0]) + b_ref[0]


def _cond(c, ada_w, ada_b):
    depth, d, d3 = ada_w.shape
    bsz = c.shape[0]
    bp = -(-bsz // SUBLANES) * SUBLANES
    c8 = jnp.pad(c, ((0, bp - bsz), (0, 0)))
    tn = d
    out = pl.pallas_call(
        _cond_kernel,
        grid=(depth, d3 // tn),
        in_specs=[pl.BlockSpec((bp, d), lambda i, j: (0, 0)),
                  pl.BlockSpec((1, d, tn), lambda i, j: (i, 0, j)),
                  pl.BlockSpec((1, 1, tn), lambda i, j: (i, 0, j))],
        out_specs=pl.BlockSpec((1, bp, tn), lambda i, j: (i, 0, j)),
        out_shape=jax.ShapeDtypeStruct((depth, bp, d3), F32),
        name="adaln_cond",
    )(c8, ada_w, ada_b.reshape(depth, 1, d3))
    return out[:, :bsz].reshape(depth, bsz, 3, d)


def _lru_kernel(x_ref, cond_ref, ng_ref, inw_ref, cw_ref, cb_ref, gw_ref,
                gb_ref, lam_ref, outw_ref, o_ref, tail_ref, carry_ref, *, nb, ns, tm, d):
    t = pl.program_id(1)

    @pl.when(t == 0)
    def _():
        tail_ref[...] = jnp.zeros_like(tail_ref)
        carry_ref[...] = jnp.zeros_like(carry_ref)

    ncol = d // LRU_COL_BLOCK
    tiles = [(b, s) for s in range(ns) for b in range(nb)]

    def permuted_input(b, s):
        x = x_ref[b, pl.ds(s * tm, tm), :]
        h = _modulated_rmsnorm(x, ng_ref[...], cond_ref[b, 0:1, :], cond_ref[b, 1:2, :])
        return jnp.swapaxes(h.reshape(SUBLANES, tm // SUBLANES, d), 0, 1).reshape(
            tm, d).astype(BF16)

    def in_proj(hp, n):
        cols = slice(n * LRU_COL_BLOCK, (n + 1) * LRU_COL_BLOCK)
        zcols = slice(d + n * LRU_COL_BLOCK, d + (n + 1) * LRU_COL_BLOCK)
        return (jnp.dot(hp, inw_ref[:, cols], preferred_element_type=F32),
                jnp.dot(hp, inw_ref[:, zcols], preferred_element_type=F32))

    def out_proj(tile, yn, n):
        b, s = tile
        rows = pl.ds(s * tm, tm)
        cols = slice(n * LRU_COL_BLOCK, (n + 1) * LRU_COL_BLOCK)
        out = jnp.dot(yn, outw_ref[:, cols], preferred_element_type=F32)
        o_ref[b, rows, cols] = x_ref[b, rows, cols] + cond_ref[b, 2:3, cols] * out

    def unpermute(y_blocks):
        y = jnp.concatenate(y_blocks, axis=1)
        return jnp.swapaxes(y.reshape(tm // SUBLANES, SUBLANES, d), 0, 1).reshape(
            tm, d).astype(BF16)

    hp = permuted_input(*tiles[0])
    proj = [in_proj(hp, n) for n in range(ncol)]
    prev, yn_prev = None, None
    for idx, (b, s) in enumerate(tiles):
        nxt = tiles[idx + 1] if idx + 1 < len(tiles) else None
        if nxt is not None:
            hp = permuted_input(*nxt)
        proj_next, y_blocks = [], []
        def recurrence(n, half):
            lo = half * LRU_BLOCK_W
            cols = slice(n * LRU_COL_BLOCK + lo, n * LRU_COL_BLOCK + lo + LRU_BLOCK_W)
            blk = n * (LRU_COL_BLOCK // LRU_BLOCK_W) + half
            y_blocks.append(_lru_recurrence_block(
                proj[n][0][:, lo:lo + LRU_BLOCK_W], proj[n][1][:, lo:lo + LRU_BLOCK_W],
                cw_ref.at[:, cols], cb_ref.at[:, cols], gw_ref.at[pl.ds(blk, 1)],
                gb_ref.at[:, cols], lam_ref.at[:, cols], tail_ref.at[b, :, cols],
                carry_ref.at[b, :, cols], tm=tm))

        for n in range(ncol):
            if nxt is not None:
                proj_next.append(in_proj(hp, n))
            recurrence(n, 0)
            if prev is not None:
                out_proj(prev, yn_prev, n)
            recurrence(n, 1)
        proj = proj_next
        prev, yn_prev = (b, s), unpermute(y_blocks)
    for n in range(ncol):
        out_proj(prev, yn_prev, n)


def _lru_recurrence_block(xb, zg, cw_ref, cb_ref, gw_ref, gb_ref, lam_ref, tail_ref, carry_ref, *, tm):
    g_rows = tm // SUBLANES
    w = xb.shape[1]
    n_tail = (CONV_WIDTH - 1) * SUBLANES
    cur_tail = xb[tm - n_tail:, :]
    prev_tail = tail_ref[...]
    sub = lax.broadcasted_iota(jnp.int32, (SUBLANES, w), 0)
    heads = []
    for m in range(CONV_WIDTH - 1):
        cur = pltpu.roll(cur_tail[m * SUBLANES:(m + 1) * SUBLANES, :], 1, 0)
        prv = pltpu.roll(prev_tail[m * SUBLANES:(m + 1) * SUBLANES, :], 1, 0)
        heads.append(jnp.where(sub == 0, prv, cur))
    tail_ref[...] = cur_tail
    ext = jnp.concatenate(heads + [xb], axis=0)
    xf = cb_ref[...]
    for k in range(CONV_WIDTH):
        off = n_tail - k * SUBLANES
        xf = xf + cw_ref[CONV_WIDTH - 1 - k:CONV_WIDTH - k, :] * ext[off:off + tm, :]

    xfb = xf.astype(BF16)
    pre_r, pre_i = [], []
    for n in range(w // LRU_BLOCK_W):
        pre = jnp.dot(xfb[:, n * LRU_BLOCK_W:(n + 1) * LRU_BLOCK_W], gw_ref[n],
                      preferred_element_type=F32)
        pre_r.append(pre[:, :LRU_BLOCK_W])
        pre_i.append(pre[:, LRU_BLOCK_W:])
    tanh_r = jnp.tanh(jnp.concatenate(pre_r, axis=1) + gb_ref[0:1, :])
    tanh_i = jnp.tanh(jnp.concatenate(pre_i, axis=1) + gb_ref[1:2, :])
    half_c = (-0.5 * RG_C) * _softplus(-lam_ref[...])
    log_a = half_c * tanh_r + half_c
    a_t = jnp.exp(log_a)
    th = jnp.tanh(log_a)
    msq = -0.5 * th / (1.0 - th)
    half_mult = jnp.where(msq > 0.0, msq * lax.rsqrt(msq), 0.0)
    b_t = (half_mult * xf) * (tanh_i + 1.0)

    h_loc = jnp.zeros((SUBLANES, w), F32)
    p_loc = jnp.ones((SUBLANES, w), F32)
    h_rows, p_rows = [], []
    for i in range(g_rows):
        a_i = a_t[i * SUBLANES:(i + 1) * SUBLANES, :]
        h_loc = a_i * h_loc + b_t[i * SUBLANES:(i + 1) * SUBLANES, :]
        p_loc = a_i * p_loc
        h_rows.append(h_loc)
        p_rows.append(p_loc)
    c_in = jnp.broadcast_to(carry_ref[SUBLANES - 1:SUBLANES, :], (SUBLANES, w))
    c_sub = c_in
    for _ in range(SUBLANES - 1):
        c_sub = jnp.where(sub == 0, c_in, pltpu.roll(h_loc + p_loc * c_sub, 1, 0))
    carry_ref[...] = h_loc + p_loc * c_sub
    hseq = jnp.concatenate([hr + pr * c_sub for hr, pr in zip(h_rows, p_rows)], axis=0)
    return (hseq * zg) * (jnp.tanh(zg) + 1.0)


def _lru_layer(x, cond_i, norm_g, in_w_all, conv_w, conv_b, gw_all, gate_b, lam, out_w_all, j, *, nb,
               ns, tm):
    bsz, s, d = x.shape
    nblk = d // LRU_BLOCK_W
    const = lambda *shape: pl.BlockSpec(shape, lambda b, t: (0,) * len(shape))
    layer = lambda *shape: pl.BlockSpec((None,) + shape, lambda b, t: (j,) + (0,) * len(shape))
    kern = functools.partial(_lru_kernel, nb=nb, ns=ns, tm=tm, d=d)
    return pl.pallas_call(
        kern,
        grid=(bsz // nb, s // (ns * tm)),
        in_specs=[pl.BlockSpec((nb, ns * tm, d), lambda b, t: (b, t, 0)),
                  pl.BlockSpec((nb, 3, d), lambda b, t: (b, 0, 0)),
                  const(1, d), layer(d, 2 * d), const(CONV_WIDTH, d),
                  const(1, d), layer(nblk, LRU_BLOCK_W, 2 * LRU_BLOCK_W), const(2, d), const(1, d),
                  layer(d, d)],
        out_specs=pl.BlockSpec((nb, ns * tm, d), lambda b, t: (b, t, 0)),
        out_shape=jax.ShapeDtypeStruct((bsz, s, d), F32),
        scratch_shapes=[pltpu.VMEM((nb, (CONV_WIDTH - 1) * SUBLANES, d), F32),
                        pltpu.VMEM((nb, SUBLANES, d), F32)],
        compiler_params=pltpu.CompilerParams(dimension_semantics=("arbitrary", "arbitrary"),
                                             vmem_limit_bytes=VMEM_LIMIT_BYTES),
        name="rglru_layer",
    )(x, cond_i, norm_g.reshape(1, d), in_w_all, conv_w, conv_b.reshape(1, d), gw_all, gate_b,
      lam.reshape(1, d), out_w_all)


def _gdn_kernel(x_ref, cond_ref, ng_ref, wqkvz_ref, wab_ref, wabt_ref, cw_ref, alog_ref,
                dtb_ref, alogr_ref, dtbr_ref, tri_ref, trit_ref, og_ref, outw_ref, fg_ref, o_ref,
                hist_ref, state_ref, qp_ref, kp_ref, vp_ref,
                *, nb, ns, tm, d, nheads, final_norm):
    c_len = GDN_CHUNK
    nchunks = tm // c_len
    npairs = nheads // 2
    n_mat = nchunks * npairs
    hk = nheads * HEAD_DIM
    n_tail = (CONV_WIDTH - 1) * SUBLANES
    t = pl.program_id(1)

    @pl.when(t == 0)
    def _():
        hist_ref[...] = jnp.zeros_like(hist_ref)
        state_ref[...] = jnp.zeros_like(state_ref)

    rr = lax.broadcasted_iota(jnp.int32, (2 * c_len, 2 * c_len), 0)
    cc = lax.broadcasted_iota(jnp.int32, (2 * c_len, 2 * c_len), 1)
    tr = _chunk_time(rr & (c_len - 1))
    tc = _chunk_time(cc & (c_len - 1))
    same = (rr >= c_len) == (cc >= c_len)
    causal = same & (tr >= tc)
    strict = same & (tr > tc)
    eye = (rr == cc).astype(F32)
    merge_masks = []
    blk = 1
    while blk < c_len:
        merge_masks.append(same & ((tr ^ tc) < 2 * blk) & ((tr & blk) != 0) & ((tc & blk) == 0))
        blk *= 2
    sub = lax.broadcasted_iota(jnp.int32, (SUBLANES, GDN_COL_BLOCK), 0)

    def stage_a(tile, slot, ctx):
        b, s = tile
        rows = pl.ds(s * tm, tm)
        h = _modulated_rmsnorm(x_ref[b, rows, :], ng_ref[...], cond_ref[b, 0:1, :],
                               cond_ref[b, 1:2, :])
        hb = jnp.swapaxes(h.reshape(nchunks, SUBLANES, SUBLANES, d), 1, 2).reshape(
            tm, d).astype(BF16)
        yield
        z_blocks = []
        for j in range(hk // GDN_COL_BLOCK):
            zcols = slice(3 * hk + j * GDN_COL_BLOCK, 3 * hk + (j + 1) * GDN_COL_BLOCK)
            z_blocks.append(jnp.dot(hb, wqkvz_ref[:, zcols], preferred_element_type=F32))
            yield
        ctx["z"] = jnp.concatenate(z_blocks, axis=1)
        heads_per_piece = GDN_COL_BLOCK // HEAD_DIM
        for j in range(3 * hk // GDN_COL_BLOCK):
            cols = slice(j * GDN_COL_BLOCK, (j + 1) * GDN_COL_BLOCK)
            pre = jnp.dot(hb, wqkvz_ref[:, cols], preferred_element_type=F32)
            prev_tail = hist_ref[b, :, cols]
            conv_chunks = []
            for c in range(nchunks):
                cur = pre[c * c_len:(c + 1) * c_len, :]
                cur_tail = cur[c_len - n_tail:, :]
                heads = []
                for m in range(CONV_WIDTH - 1):
                    rows8 = slice(m * SUBLANES, (m + 1) * SUBLANES)
                    heads.append(jnp.where(sub == 0, pltpu.roll(prev_tail[rows8, :], 1, 0),
                                           pltpu.roll(cur_tail[rows8, :], 1, 0)))
                ext = jnp.concatenate(heads + [cur], axis=0)
                conv_c = cw_ref[CONV_WIDTH - 1:CONV_WIDTH, cols] * cur
                for k in range(1, CONV_WIDTH):
                    off = n_tail - k * SUBLANES
                    conv_c = conv_c + (cw_ref[CONV_WIDTH - 1 - k:CONV_WIDTH - k, cols]
                                       * ext[off:off + c_len, :])
                conv_chunks.append(conv_c)
                prev_tail = cur_tail
            hist_ref[b, :, cols] = prev_tail
            conv = jnp.concatenate(conv_chunks, axis=0)
            act = conv * (jnp.tanh(conv) + 1.0)
            for i in range(heads_per_piece):
                which, h = divmod(j * heads_per_piece + i, nheads)
                a_h = act[:, i * HEAD_DIM:(i + 1) * HEAD_DIM]
                if which < 2:
                    inv = lax.rsqrt(jnp.sum(a_h * a_h, axis=-1, keepdims=True) + EPS)
                    a_h = a_h * (inv * (HEAD_DIM ** -0.5) if which == 0 else inv)
                dst_ref = (qp_ref, kp_ref, vp_ref)[which]
                hp, par = divmod(h, 2)
                for c in range(nchunks):
                    dst_ref[slot, c, hp, par * c_len:(par + 1) * c_len, :] = (
                        a_h[c * c_len:(c + 1) * c_len, :])
            yield
        ab = jnp.dot(hb, wab_ref[...], preferred_element_type=F32)
        abt = lax.dot_general(wabt_ref[...], hb, (((1,), (1,)), ((), ())),
                              preferred_element_type=F32)
        g_col = -jnp.exp(alog_ref[...]) * _softplus(ab + dtb_ref[...])
        g_row = -jnp.exp(alogr_ref[...]) * _softplus(abt + dtbr_ref[...])
        ctx["beta"] = _sigmoid(ab)
        gcol_parts = _split3(g_col)
        grow_parts = _split3(g_row)
        gc_blocks, gcr_blocks = [], []
        for i in range(tm // LANES):
            sl = slice(i * LANES, (i + 1) * LANES)
            gc_blocks.append(sum(jnp.dot(tri_ref[...], p[sl, :], preferred_element_type=F32)
                                 for p in gcol_parts))
            gcr_blocks.append(sum(jnp.dot(p[:, sl], trit_ref[...], preferred_element_type=F32)
                                  for p in grow_parts))
        ctx["gc"] = jnp.concatenate(gc_blocks, axis=0)
        ctx["gcr"] = jnp.concatenate(gcr_blocks, axis=1)
        yield

    def pair_cols(arr, lane0=0):
        mats = []
        for c in range(nchunks):
            blk_rows = arr[c * c_len:(c + 1) * c_len, :]
            for hp in range(npairs):
                lanes = [lane0 + 2 * hp + par for par in range(2)]
                mats.append(jnp.concatenate(
                    [jnp.broadcast_to(blk_rows[:, ln:ln + 1], (c_len, HEAD_DIM)) for ln in lanes],
                    axis=0))
        return jnp.stack(mats)

    def pair_rows(arr):
        mats = []
        for c in range(nchunks):
            for hp in range(npairs):
                mats.append(jnp.concatenate(
                    [arr[2 * hp + par:2 * hp + par + 1, c * c_len:(c + 1) * c_len]
                     for par in range(2)], axis=1))
        return jnp.stack(mats)

    def merge_level(t_mat, a_mat, m, blk):
        n = 2 * c_len
        if blk >= SUBLANES:
            return t_mat - _bmm(t_mat, _bmm(jnp.where(m[None], a_mat, 0.0), t_mat))
        vrows = [slice(v * SUBLANES, (v + 1) * SUBLANES) for v in range(n // SUBLANES)]
        late = [(v & blk) != 0 for v in range(n // SUBLANES)]
        pick = lambda arr: jnp.concatenate([arr[:, sl] for sl, lt in zip(vrows, late) if lt], axis=1)
        o_lo = jnp.where(pick(m[None]), pick(a_mat), 0.0)
        x_lo = _bmm(o_lo, t_mat)
        zeros = jnp.zeros((n_mat, SUBLANES, n), F32)
        pieces, idx = [], 0
        for lt in late:
            pieces.append(x_lo[:, idx * SUBLANES:(idx + 1) * SUBLANES] if lt else zeros)
            idx += lt
        upd = _bmm(pick(t_mat), jnp.concatenate(pieces, axis=1))
        pieces, idx = [], 0
        for sl, lt in zip(vrows, late):
            pieces.append(t_mat[:, sl] - upd[:, idx * SUBLANES:(idx + 1) * SUBLANES] if lt
                          else t_mat[:, sl])
            idx += lt
        return jnp.concatenate(pieces, axis=1)

    def stage_b(tile, slot, ctx):
        b, s = tile
        rows = pl.ds(s * tm, tm)
        q2 = qp_ref[slot].reshape(n_mat, 2 * c_len, HEAD_DIM)
        k2 = kp_ref[slot].reshape(n_mat, 2 * c_len, HEAD_DIM)
        v2 = vp_ref[slot].reshape(n_mat, 2 * c_len, HEAD_DIM)
        beta2 = pair_cols(ctx["beta"], lane0=nheads)
        gcol2 = pair_cols(ctx["gc"])
        grow2 = pair_rows(ctx["gcr"])
        yield
        eg2 = jnp.exp(gcol2)
        glast2 = jnp.concatenate(
            [jnp.broadcast_to(gcol2[:, (par + 1) * c_len - 1:(par + 1) * c_len, :],
                              (n_mat, c_len, HEAD_DIM)) for par in range(2)], axis=1)
        kd2 = jnp.exp(glast2 - gcol2)
        decay = jnp.exp(jnp.where(causal[None], gcol2 - grow2, -1e30))
        yield
        kb2 = k2 * beta2
        gram = _bmm_nt(jnp.concatenate([kb2, q2], axis=1), k2)
        a_mat = jnp.where(strict[None], gram[:, :2 * c_len] * decay, 0.0)
        attn = gram[:, 2 * c_len:] * decay
        yield
        t_mat = eye[None] - jnp.where(merge_masks[0][None], a_mat, 0.0)
        for lvl, m in enumerate(merge_masks[1:], start=1):
            t_mat = merge_level(t_mat, a_mat, m, 2 ** lvl)
            yield
        wu = _bmm(t_mat, jnp.concatenate([kb2 * eg2, v2 * beta2], axis=2))
        w2 = wu[:, :, :HEAD_DIM]
        u2 = wu[:, :, HEAD_DIM:]
        qd2 = q2 * eg2
        kdec2 = k2 * kd2
        yield
        st = slice(b * npairs, (b + 1) * npairs)
        o_chunks = []
        for c in range(nchunks):
            sl = slice(c * npairs, (c + 1) * npairs)
            vn_parts, qs_parts = [], []
            for par in range(2):
                rs = slice(par * c_len, (par + 1) * c_len)
                wq = _bmm(jnp.concatenate([w2[sl, rs], qd2[sl, rs]], axis=1), state_ref[st, par])
                vn_parts.append(u2[sl, rs] - wq[:, :c_len])
                qs_parts.append(wq[:, c_len:])
            v_new = jnp.concatenate(vn_parts, axis=1)
            o_chunks.append(jnp.concatenate(qs_parts, axis=1) + _bmm(attn[sl], v_new))
            for par in range(2):
                rs = slice(par * c_len, (par + 1) * c_len)
                e_last = jnp.broadcast_to(eg2[sl, (par + 1) * c_len - 1:(par + 1) * c_len, :],
                                          (npairs, HEAD_DIM, HEAD_DIM))
                state_ref[st, par] = (state_ref[st, par] * e_last
                                      + _bmm_tn(kdec2[sl, rs], v_new[:, rs]))
            yield
        parts = []
        for h in range(nheads):
            hp, par = divmod(h, 2)
            oh = jnp.concatenate([oc[hp, par * c_len:(par + 1) * c_len, :] for oc in o_chunks],
                                 axis=0)
            parts.append(oh * lax.rsqrt(jnp.mean(oh * oh, axis=-1, keepdims=True) + EPS))
        z = ctx["z"]
        og = jnp.concatenate(parts, axis=1) * og_ref[...] * (z * (jnp.tanh(z) + 1.0))
        og = jnp.swapaxes(og.reshape(nchunks, SUBLANES, SUBLANES, hk), 1, 2).reshape(
            tm, hk).astype(BF16)
        yield
        xn_blocks = []
        for j in range(d // GDN_COL_BLOCK):
            cols = slice(j * GDN_COL_BLOCK, (j + 1) * GDN_COL_BLOCK)
            out = jnp.dot(og, outw_ref[:, cols], preferred_element_type=F32)
            xn = x_ref[b, rows, cols] + cond_ref[b, 2:3, cols] * out
            if final_norm:
                xn_blocks.append(xn)
            else:
                o_ref[b, rows, cols] = xn
            yield
        if final_norm:
            xn = jnp.concatenate(xn_blocks, axis=1)
            ms = jnp.mean(xn * xn, axis=-1, keepdims=True)
            o_ref[b, rows, :] = xn * lax.rsqrt(ms + EPS) * fg_ref[...]

    tiles = [(b, s) for s in range(ns) for b in range(nb)]
    ctxs = [dict() for _ in tiles]
    order = "abab" + "a" + "ab" + "a" + "ab" * 10 + "aaa"
    _interleave("", a=stage_a(tiles[0], 0, ctxs[0]))
    for idx, tile in enumerate(tiles):
        gens = dict(b=stage_b(tile, idx % 2, ctxs[idx]))
        if idx + 1 < len(tiles):
            gens["a"] = stage_a(tiles[idx + 1], (idx + 1) % 2, ctxs[idx + 1])
        _interleave(order, **gens)


def _chunk_time(p):
    assert SUBLANES == 8 and GDN_CHUNK == 64
    return ((p & 7) << 3) | (p >> 3)


def _pair_tri():
    r = np.arange(LANES)
    tt = _chunk_time(r % GDN_CHUNK)
    m = (tt[:, None] >= tt[None, :]) & (r[:, None] // GDN_CHUNK == r[None, :] // GDN_CHUNK)
    return m.astype(np.float32)


def _gdn_layer(x, cond_i, norm_g, wqkvz_all, w_ab, conv_w, a_log, dt_bias, onorm_g, out_w_all,
               final_g, j, *, nb, ns, tm, final_norm):
    bsz, s, d = x.shape
    nheads = a_log.shape[0]
    hk = nheads * HEAD_DIM
    nchunks = tm // GDN_CHUNK
    npairs = nheads // 2
    o1 = 3 * hk
    wab = jnp.pad(w_ab, ((0, 0), (0, LANES - 2 * nheads))).astype(BF16)
    wabt = w_ab.T.astype(BF16)
    lane_vec = lambda v: jnp.pad(v, (0, LANES - nheads)).reshape(1, LANES)
    row_vec = lambda v: jnp.broadcast_to(jnp.pad(v, (0, nheads))[:, None], (2 * nheads, tm))
    tri = _pair_tri()
    const = lambda *shape: pl.BlockSpec(shape, lambda b, t: (0,) * len(shape))
    layer = lambda *shape: pl.BlockSpec((None,) + shape, lambda b, t: (j,) + (0,) * len(shape))
    kern = functools.partial(_gdn_kernel, nb=nb, ns=ns, tm=tm, d=d, nheads=nheads,
                             final_norm=final_norm)
    pair_stack = pltpu.VMEM((2, nchunks, npairs, 2 * GDN_CHUNK, HEAD_DIM), F32)
    return pl.pallas_call(
        kern,
        grid=(bsz // nb, s // (ns * tm)),
        in_specs=[pl.BlockSpec((nb, ns * tm, d), lambda b, t: (b, t, 0)),
                  pl.BlockSpec((nb, 3, d), lambda b, t: (b, 0, 0)),
                  const(1, d), layer(d, wqkvz_all.shape[2]), const(d, LANES),
                  const(2 * nheads, d), const(CONV_WIDTH, o1), const(1, LANES), const(1, LANES),
                  const(2 * nheads, tm), const(2 * nheads, tm), const(LANES, LANES),
                  const(LANES, LANES), const(1, hk), layer(hk, d), const(1, d)],
        out_specs=pl.BlockSpec((nb, ns * tm, d), lambda b, t: (b, t, 0)),
        out_shape=jax.ShapeDtypeStruct((bsz, s, d), F32),
        scratch_shapes=[pltpu.VMEM((nb, (CONV_WIDTH - 1) * SUBLANES, o1), F32),
                        pltpu.VMEM((nb * npairs, 2, HEAD_DIM, HEAD_DIM), F32),
                        pair_stack, pair_stack, pair_stack],
        compiler_params=pltpu.CompilerParams(dimension_semantics=("arbitrary", "arbitrary"),
                                             vmem_limit_bytes=VMEM_LIMIT_BYTES),
        name="gdn_layer",
    )(x, cond_i, norm_g.reshape(1, d), wqkvz_all, wab, wabt, conv_w, lane_vec(a_log),
      lane_vec(dt_bias), row_vec(a_log), row_vec(dt_bias), jnp.asarray(tri, BF16),
      jnp.asarray(tri.T, BF16), jnp.tile(onorm_g, nheads).reshape(1, hk), out_w_all,
      final_g.reshape(1, d))


def kernel(x, c, ada_w, ada_b, norm_g, final_g, lru_in_w, lru_conv_w, lru_conv_b, lru_gate_w,
           lru_gate_b, lru_lambda, lru_out_w, gdn_in_w, gdn_conv_w, gdn_a_log, gdn_dt_bias,
           gdn_onorm_g, gdn_out_w):
    depth = ada_w.shape[0]
    assert depth % 2 == 0, "layers alternate RG-LRU / DeltaNet and the last one applies the final norm"
    cond = _cond(c, ada_w, ada_b)
    d = x.shape[-1]
    n_hv = gdn_a_log.shape[1] * HEAD_DIM
    n_qkv = 3 * n_hv
    halve_tail = lambda n_keep, n_all: jnp.where(jnp.arange(n_all) < n_keep, 1.0, 0.5).astype(F32)
    lru_in_bf = (lru_in_w * halve_tail(d, lru_in_w.shape[-1])).astype(BF16)
    lru_out_bf = lru_out_w.astype(BF16)
    lru_gw_bf = (0.5 * jnp.concatenate([lru_gate_w[:, 0], lru_gate_w[:, 1]], axis=-1)).astype(BF16)
    z_scale = jnp.where((jnp.arange(gdn_in_w.shape[-1]) >= n_qkv)
                        & (jnp.arange(gdn_in_w.shape[-1]) < n_qkv + n_hv), 0.5, 1.0).astype(F32)
    gdn_in_bf = (gdn_in_w * z_scale).astype(BF16)
    gdn_out_bf = gdn_out_w.astype(BF16)
    for i in range(depth):
        j = i // 2
        if i % 2 == 0:
            x = _lru_layer(x, cond[i], norm_g[i], lru_in_bf, lru_conv_w[j], lru_conv_b[j], lru_gw_bf,
                           0.5 * lru_gate_b[j], lru_lambda[j], lru_out_bf, j, nb=1, ns=4, tm=256)
        else:
            x = _gdn_layer(x, cond[i], norm_g[i], gdn_in_bf, gdn_in_w[j, :, n_qkv + n_hv:],
                           0.5 * gdn_conv_w[j], gdn_a_log[j], gdn_dt_bias[j], gdn_onorm_g[j],
                           gdn_out_bf, final_g, j, nb=1, ns=2, tm=256, final_norm=(i == depth - 1))
    return x
```
